```python
import math
import jax, jax.numpy as jnp
from jax import lax
import numpy as np

D_MODEL = 1024
BATCH = 32
SEQ = 256
DEPTH = 2
DEC_BATCH = 2
DEC_SEQ = 2048
PAST_LEN = 256

GRID_W = 64
EPS = 1e-6
H_A = 4
DK_A = 128
DV_A = 128
W_A = H_A * DV_A
CHUNK_A = 16
H_B = 4
DH_B = 64
DV_B = 2 * DH_B
W_B = H_B * DV_B
Q_BLOCK = 128
ROPE_BASE = 10000.0
G_C = 4
DG_C = 128
W_C = G_C * DG_C
N_BRANCH = 3
IN_SIZES = (H_A * DK_A, H_A * DK_A, H_A * DK_A, W_A, W_A,
            H_B * 2 * DH_B, H_B * 2 * DH_B, W_B, W_B,
            W_C, W_C, N_BRANCH * D_MODEL)
D_IN = 3 * H_A * DK_A + 2 * W_A + 4 * H_B * DH_B + 2 * W_B + 2 * W_C + N_BRANCH * D_MODEL

kernel_name = 'hybrid_hgrn2_diffattn_fnet_dit_step'


def rms_norm(x, g):
    xf = x.astype(jnp.float32)
    y = xf * lax.rsqrt(jnp.mean(xf * xf, axis=-1, keepdims=True) + EPS)
    return (y * g.astype(jnp.float32)).astype(x.dtype)


def rope_angles(pos):
    nf = DH_B // 4
    inv = ROPE_BASE ** (-jnp.arange(nf, dtype=jnp.float32) / nf)
    ang = pos[:, None] * inv[None, :]
    return jnp.cos(ang), jnp.sin(ang)


def axial_rope(x, rope):
    (cos_r, sin_r), (cos_c, sin_c) = rope
    half = DH_B // 2
    nf = half // 2
    xf = x.astype(jnp.float32)

    def rot(xa, cos, sin):
        cos = cos[None, :, None, None, :]
        sin = sin[None, :, None, None, :]
        x1, x2 = xa[..., :nf], xa[..., nf:]
        return jnp.concatenate([x1 * cos - x2 * sin, x1 * sin + x2 * cos], axis=-1)

    out = jnp.concatenate([rot(xf[..., :half], cos_r, sin_r),
                           rot(xf[..., half:], cos_c, sin_c)], axis=-1)
    return out.astype(x.dtype)


def hgrn2_chunked(q, k, logf, v, s0):
    B, T, H, _ = q.shape
    C = CHUNK_A
    N = T // C

    def chunks(a):
        return a.astype(jnp.float32).reshape(B, N, C, H, a.shape[-1])

    q, k, logf, v = chunks(q), chunks(k), chunks(logf), chunks(v)
    G = jnp.cumsum(logf, axis=2)
    lower = jnp.tril(jnp.ones((C, C), dtype=bool))[None, None, :, :, None, None]
    decay = jnp.exp(jnp.where(lower, G[:, :, :, None] - G[:, :, None, :], -jnp.inf))
    attn = jnp.einsum('bnthk,bnshk,bntshk->bnhts', q, k, decay)
    o_intra = jnp.einsum('bnhts,bnshv->bnthv', attn, v)
    g_end = G[:, :, -1]
    u = jnp.einsum('bnshk,bnshv->bnhkv', k * jnp.exp(g_end[:, :, None] - G), v)

    def step(s, inp):
        d_n, u_n = inp
        return d_n[..., None] * s + u_n, s

    s_fin, s_start = lax.scan(step, s0, (jnp.moveaxis(jnp.exp(g_end), 1, 0),
                                         jnp.moveaxis(u, 1, 0)))
    o_inter = jnp.einsum('bnthk,nbhkv->bnthv', q * jnp.exp(G), s_start)
    return (o_intra + o_inter).reshape(B, T, H, -1), s_fin


def diff_attention(q, k, v, lam):
    B, Tq = q.shape[0], q.shape[1]
    qb = jnp.moveaxis(q.reshape(B, Tq // Q_BLOCK, Q_BLOCK, H_B, 2, DH_B), 1, 0)

    def block(q_blk):
        s = jnp.einsum('bqhmd,bkhmd->bhmqk', q_blk, k).astype(jnp.float32)
        p = jax.nn.softmax(s, axis=-1)
        w = (p[:, :, 0] - lam * p[:, :, 1]).astype(v.dtype)
        return jnp.einsum('bhqk,bkhe->bqhe', w, v)

    o = lax.map(block, qb)
    return jnp.moveaxis(o, 0, 1).reshape(B, Tq, H_B, DV_B)


def fourier_mix(u):
    B, T, _ = u.shape
    ug = u.astype(jnp.float32).reshape(B, T, G_C, DG_C)
    f = jnp.fft.fft2(ug, axes=(1, 3), norm='ortho').real
    return f.reshape(B, T, W_C).astype(u.dtype)


def trunk_layer(x, cond, layer, lb, lw, ctx=None, rope=None):
    (ada_w, ada_b, norm_g, w_in, hgrn_g, qn_g, kn_g, lam_p, sub_g,
     w_br_a, w_br_b, w_br_c, w_out) = lw
    B, T, _ = x.shape
    mod = jax.nn.silu(cond) @ ada_w + ada_b
    shift, scale, gate = jnp.split(mod, 3, axis=-1)
    h = rms_norm(x, norm_g) * (1.0 + scale[:, None, :]) + shift[:, None, :]
    split_idx = np.cumsum(IN_SIZES)[:-1].tolist()
    qa, ffa, fba, ia, za, qb, kb, vb, zb, uc, zc, gts = jnp.split(h @ w_in, split_idx, axis=-1)

    q_a = jax.nn.silu(qa).reshape(B, T, H_A, DK_A)
    i_a = ia.reshape(B, T, H_A, DV_A)
    if ctx is None:
        s0 = jnp.zeros((B, 2, H_A, DK_A, DV_A), jnp.float32)
    else:
        s0 = ctx[2].astype(jnp.float32)
    o_dirs = []
    finals = []
    for d, fraw in enumerate((ffa, fba)):
        lbd = lb[d]
        logf = jnp.logaddexp(jnp.log(lbd), jnp.log1p(-lbd) + jax.nn.log_sigmoid(fraw.astype(jnp.float32)))
        logf = logf.reshape(B, T, H_A, DK_A)
        k_a = -jnp.expm1(logf)
        if d == 0:
            o_d, s_fin = hgrn2_chunked(q_a, k_a, logf, i_a, s0[:, 0])
        else:
            o_d, s_fin = hgrn2_chunked(q_a[:, ::-1], k_a[:, ::-1], logf[:, ::-1], i_a[:, ::-1], s0[:, 1])
            o_d = o_d[:, ::-1]
        o_dirs.append(o_d)
        finals.append(s_fin)
    o_a = rms_norm(o_dirs[0] + o_dirs[1], hgrn_g.reshape(H_A, DV_A)).astype(x.dtype)
    y_a = o_a.reshape(B, T, W_A) * jax.nn.silu(za)

    q_b = rms_norm(qb.reshape(B, T, H_B, 2, DH_B), qn_g)
    k_b = rms_norm(kb.reshape(B, T, H_B, 2, DH_B), kn_g)
    v_b = vb.reshape(B, T, H_B, DV_B)
    if ctx is None:
        k_all, v_all = k_b, v_b
    else:
        q_b = axial_rope(q_b, rope)
        k_all = jnp.concatenate([axial_rope(k_b, rope), ctx[0].astype(k_b.dtype)], axis=1)
        v_all = jnp.concatenate([v_b, ctx[1].astype(v_b.dtype)], axis=1)
    lam_init = 0.8 - 0.6 * math.exp(-0.3 * layer)
    lp = lam_p.astype(jnp.float32)
    lam = jnp.exp(jnp.sum(lp[0] * lp[1])) - jnp.exp(jnp.sum(lp[2] * lp[3])) + lam_init
    o_b = diff_attention(q_b * (DH_B ** -0.5), k_all, v_all, lam)
    y_b = (rms_norm(o_b, sub_g) * (1.0 - lam_init)).reshape(B, T, W_B) * jax.nn.silu(zb)

    y_c = fourier_mix(uc) * jax.nn.silu(zc)

    g = jax.nn.sigmoid(gts).reshape(B, T, N_BRANCH, D_MODEL)
    merged = (g[:, :, 0] * (y_a @ w_br_a) + g[:, :, 1] * (y_b @ w_br_b)
              + g[:, :, 2] * (y_c @ w_br_c))
    x_new = x + gate[:, None, :] * (merged @ w_out)
    if ctx is None:
        return x_new, k_b, v_b, jnp.stack(finals, axis=1)
    return x_new


def setup_inputs(seed: int = 0) -> dict:
    key = jax.random.key(seed)
    ks = jax.random.split(key, 21)
    f32 = jnp.float32
    n = lambda k, s: jax.random.normal(k, s, f32)
    return {
        'x_prompt': n(ks[0], (BATCH, SEQ, D_MODEL)),
        'x_sample': n(ks[1], (DEC_BATCH, DEC_SEQ, D_MODEL)),
        'c': n(ks[2], (DEC_BATCH, D_MODEL)),
        'cache_diff_k': n(ks[3], (DEC_BATCH, DEPTH, PAST_LEN, H_B, 2, DH_B)),
        'cache_diff_v': n(ks[4], (DEC_BATCH, DEPTH, PAST_LEN, H_B, DV_B)),
        'state_hgrn': 0.5 * n(ks[5], (DEC_BATCH, DEPTH, 2, H_A, DK_A, DV_A)),
        'c_ctx': n(ks[6], (D_MODEL,)),
        'ada_w': n(ks[7], (DEPTH, D_MODEL, 3 * D_MODEL)) * D_MODEL ** -0.5,
        'ada_b': 0.02 * n(ks[8], (DEPTH, 3 * D_MODEL)),
        'norm_g': 1.0 + 0.02 * n(ks[9], (DEPTH, D_MODEL)),
        'w_in': n(ks[10], (DEPTH, D_MODEL, D_IN)) * D_MODEL ** -0.5,
        'hgrn_lb': n(ks[11], (DEPTH, 2, H_A * DK_A)),
        'hgrn_norm_g': 1.0 + 0.02 * n(ks[12], (DEPTH, W_A)),
        'diff_qn_g': 1.0 + 0.02 * n(ks[13], (DEPTH, DH_B)),
        'diff_kn_g': 1.0 + 0.02 * n(ks[14], (DEPTH, DH_B)),
        'diff_lambda': 0.1 * n(ks[15], (DEPTH, 4, DH_B)),
        'diff_subln_g': 1.0 + 0.02 * n(ks[16], (DEPTH, DV_B)),
        'w_branch_a': n(ks[17], (DEPTH, W_A, D_MODEL)) * W_A ** -0.5,
        'w_branch_b': n(ks[18], (DEPTH, W_B, D_MODEL)) * W_B ** -0.5,
        'w_branch_c': n(ks[19], (DEPTH, W_C, D_MODEL)) * W_C ** -0.5,
        'w_out': n(ks[20], (DEPTH, D_MODEL, D_MODEL)) * D_MODEL ** -0.5,
    }


def reference(x_prompt, x_sample, c, cache_diff_k, cache_diff_v, state_hgrn, c_ctx,
              ada_w, ada_b, norm_g, w_in, hgrn_lb, hgrn_norm_g, diff_qn_g, diff_kn_g,
              diff_lambda, diff_subln_g, w_branch_a, w_branch_b, w_branch_c, w_out):
    lb_all = jnp.cumsum(jax.nn.softmax(hgrn_lb.astype(jnp.float32), axis=0), axis=0)
    lb_all = lb_all - lb_all[0]
    n_rows = x_sample.shape[1] // GRID_W
    pos_row = jnp.repeat(jnp.arange(n_rows, dtype=jnp.float32), GRID_W)
    pos_col = jnp.tile(jnp.arange(GRID_W, dtype=jnp.float32), n_rows)
    rope = (rope_angles(pos_row), rope_angles(pos_col))
    cond_ctx = jnp.broadcast_to(c_ctx, (x_prompt.shape[0], D_MODEL))

    xp = x_prompt
    xs = x_sample
    new_k, new_v, new_s = [], [], []
    for l in range(DEPTH):
        lw = (ada_w[l], ada_b[l], norm_g[l], w_in[l], hgrn_norm_g[l], diff_qn_g[l],
              diff_kn_g[l], diff_lambda[l], diff_subln_g[l], w_branch_a[l],
              w_branch_b[l], w_branch_c[l], w_out[l])
        xp, k_c, v_c, s_c = trunk_layer(xp, cond_ctx, l, lb_all[l], lw)
        new_k.append(k_c)
        new_v.append(v_c)
        new_s.append(s_c)
        xs = trunk_layer(xs, c, l, lb_all[l], lw,
                         ctx=(cache_diff_k[:, l], cache_diff_v[:, l], state_hgrn[:, l]),
                         rope=rope)
    new_cache_diff_k = jnp.stack(new_k, axis=1)
    new_cache_diff_v = jnp.stack(new_v, axis=1)
    new_state_hgrn = jnp.stack(new_s, axis=1)
    return (xp, xs, new_cache_diff_k, new_cache_diff_v, new_state_hgrn)
```

```python
import functools
import math

import numpy as np
import jax
import jax.numpy as jnp
from jax import lax
from jax.experimental import pallas as pl
from jax.experimental.pallas import tpu as pltpu

F32 = jnp.float32
BF16 = jnp.bfloat16

D_MODEL = 1024
DEPTH = 2
GRID_W = 64
EPS = 1e-6
H_A = 4
DK_A = 128
DV_A = 128
H_B = 4
DH_B = 64
DV_B = 128
G_C = 4
DG_C = 128
ROPE_BASE = 10000.0
SEG = 512
D_IN = 17 * SEG
QA, FFA, FBA, IA, ZA, QB, KB, VB, ZB, UC, ZC = (4 * i for i in range(11))
GATE_SEG = 11

CHUNK = 128
SUB = 16
VMEM_LIMIT = 56 * 1024 * 1024


def _cparams(sem):
    return pltpu.CompilerParams(dimension_semantics=sem, vmem_limit_bytes=VMEM_LIMIT)


def _silu(x):
    return x * jax.nn.sigmoid(x)


def _log_sigmoid(x):
    return jnp.minimum(x, 0.0) - jnp.log1p(jnp.exp(-jnp.abs(x)))


def _dot(a, b):
    return jnp.dot(a, b, preferred_element_type=F32)


def _dot_nt(a, b):
    return lax.dot_general(a, b, (((1,), (1,)), ((), ())), preferred_element_type=F32)


def _mod_kernel(c_ref, w_ref, b_ref, o_ref):
    c = c_ref[...]
    s = _silu(c).astype(BF16)
    o_ref[0] = _dot(s, w_ref[0].astype(BF16)) + b_ref[0]


def _modulation(cond8, ada_w, ada_b):
    tn = 512
    return pl.pallas_call(
        _mod_kernel,
        grid=(DEPTH, 3 * D_MODEL // tn),
        in_specs=[
            pl.BlockSpec((8, D_MODEL), lambda l, j: (0, 0)),
            pl.BlockSpec((1, D_MODEL, tn), lambda l, j: (l, 0, j)),
            pl.BlockSpec((1, 1, tn), lambda l, j: (l, 0, j)),
        ],
        out_specs=pl.BlockSpec((1, 8, tn), lambda l, j: (l, 0, j)),
        out_shape=jax.ShapeDtypeStruct((DEPTH, 8, 3 * D_MODEL), F32),
        compiler_params=_cparams(("arbitrary", "arbitrary")),
        name="adaln_mod",
    )(cond8, ada_w, ada_b.reshape(DEPTH, 1, 3 * D_MODEL))


def _proj_kernel(x_ref, mod_ref, g_ref, w_ref, la_ref, l1_ref, o_ref, h_scr):
    j = pl.program_id(1)

    @pl.when(j == 0)
    def _():
        x = x_ref[...]
        y = x * lax.rsqrt(jnp.mean(x * x, axis=-1, keepdims=True) + EPS) * g_ref[...]
        shift = mod_ref[0, 0:1, :]
        scale = mod_ref[0, 1:2, :]
        h_scr[...] = (y * (1.0 + scale) + shift).astype(BF16)

    acc = _dot(h_scr[...], w_ref[...])
    seg = j

    is_silu = (seg == 0) | (seg == 4) | (seg == 8) | (seg == 10)
    is_logf = (seg == 1) | (seg == 2)
    is_gate = seg >= GATE_SEG

    @pl.when(is_silu)
    def _():
        o_ref[...] = _silu(acc)

    @pl.when(is_logf)
    def _():
        d = seg - 1
        a = la_ref[pl.ds(d, 1), :]
        b = l1_ref[pl.ds(d, 1), :] + _log_sigmoid(acc)
        m = jnp.maximum(a, b)
        o_ref[...] = m + jnp.log1p(jnp.exp(-jnp.abs(a - b)))

    @pl.when(is_gate)
    def _():
        o_ref[...] = jax.nn.sigmoid(acc)

    @pl.when(jnp.logical_not(is_silu | is_logf | is_gate))
    def _():
        o_ref[...] = acc


def _in_proj(x2, mod3, row_of_tile, norm_g, w_in_bf, log_lb, log1m_lb, tm):
    n = x2.shape[0]
    tn = SEG
    return pl.pallas_call(
        _proj_kernel,
        grid=(n // tm, D_IN // tn),
        in_specs=[
            pl.BlockSpec((tm, D_MODEL), lambda i, j: (i, 0)),
            pl.BlockSpec((1, 3, D_MODEL), lambda i, j: (row_of_tile(i), 0, 0)),
            pl.BlockSpec((1, D_MODEL), lambda i, j: (0, 0)),
            pl.BlockSpec((D_MODEL, tn), lambda i, j: (0, j)),
            pl.BlockSpec((2, SEG), lambda i, j: (0, 0)),
            pl.BlockSpec((2, SEG), lambda i, j: (0, 0)),
        ],
        out_specs=pl.BlockSpec((tm, tn), lambda i, j: (i, j)),
        out_shape=jax.ShapeDtypeStruct((n, D_IN), F32),
        scratch_shapes=[pltpu.VMEM((tm, D_MODEL), BF16)],
        compiler_params=_cparams(("arbitrary", "arbitrary")),
        name="in_proj",
    )(x2, mod3, norm_g.reshape(1, D_MODEL), w_in_bf, log_lb, log1m_lb)


def _split3(x):
    hi = x.astype(BF16)
    r = x - hi.astype(F32)
    mid = r.astype(BF16)
    lo = (r - mid.astype(F32)).astype(BF16)
    return hi, mid, lo


def _level_exponents(g, m, reverse, diag):
    nb = CHUNK // m
    eq, ek = [], []
    ninf = jnp.full((m, 128), -jnp.inf, F32)
    for i in range(nb):
        sl = g[i * m:(i + 1) * m]
        if diag:
            r = i * m + (m // 2 - 1 if not reverse else m // 2)
            ref = g[r:r + 1]
            eq.append(sl - ref)
            ek.append(ref - sl)
        elif not reverse:
            if i % 2 == 1:
                eq.append(sl - g[i * m - 1:i * m])
                ek.append(ninf)
            else:
                eq.append(ninf)
                ek.append(g[(i + 1) * m - 1:(i + 1) * m] - sl)
        else:
            if i % 2 == 0:
                eq.append(sl - g[(i + 1) * m:(i + 1) * m + 1])
                ek.append(ninf)
            else:
                eq.append(ninf)
                ek.append(g[i * m:i * m + 1] - sl)
    return jnp.concatenate(eq, axis=0), jnp.concatenate(ek, axis=0)


def _level_masks(reverse):
    t = lax.broadcasted_iota(jnp.int32, (CHUNK, CHUNK), 0)
    s = lax.broadcasted_iota(jnp.int32, (CHUNK, CHUNK), 1)
    sh = int(math.log2(SUB))
    same = (t >> sh) == (s >> sh)
    masks = [same & ((s >= t) if reverse else (s <= t))]
    m = SUB
    while m < CHUNK:
        sh = int(math.log2(m))
        tb, sb = t >> sh, s >> sh
        if not reverse:
            masks.append(((tb & 1) == 1) & (sb == tb - 1))
        else:
            masks.append(((tb & 1) == 0) & (sb == tb + 1))
        m *= 2
    return masks


def _hgrn_kernel(*refs, n_chunks, has_state, want_state):
    it = iter(refs)
    q_ref, lff_ref, lfb_ref, v_ref, z_ref, g_ref = (next(it) for _ in range(6))
    s0_ref = next(it) if has_state else None
    y_ref = next(it)
    sfin_ref = next(it) if want_state else None
    o_scr = next(it)

    t_i = lax.broadcasted_iota(jnp.int32, (CHUNK, CHUNK), 0)
    s_i = lax.broadcasted_iota(jnp.int32, (CHUNK, CHUNK), 1)

    for d, lf_ref in enumerate((lff_ref, lfb_ref)):
        reverse = d == 1
        tri = jnp.where((s_i >= t_i) if reverse else (s_i <= t_i), 1.0, 0.0).astype(BF16)
        masks = _level_masks(reverse)
        end_row = 0 if reverse else CHUNK - 1

        def chunk_body(ci, st, lf_ref=lf_ref, reverse=reverse, tri=tri, masks=masks,
                       end_row=end_row, d=d):
            c = (n_chunks - 1 - ci) if reverse else ci
            r0 = pl.multiple_of(c * CHUNK, CHUNK)
            lf = lf_ref[0, pl.ds(r0, CHUNK), :]
            q = q_ref[0, pl.ds(r0, CHUNK), :]
            v = v_ref[0, pl.ds(r0, CHUNK), :]
            k = 1.0 - jnp.exp(lf)
            hi, mid, lo = _split3(lf)
            g = _dot(tri, hi) + _dot(tri, mid) + _dot(tri, lo)
            g_end = g[end_row:end_row + 1]
            v_bf = v.astype(BF16)

            qg = (q * jnp.exp(g)).astype(BF16)
            o = _dot_nt(qg, st.astype(BF16))

            a = jnp.zeros((CHUNK, CHUNK), F32)
            m = SUB
            for lvl, mask in enumerate(masks):
                eq, ek = _level_exponents(g, m, reverse, diag=(lvl == 0))
                p = _dot_nt((q * jnp.exp(eq)).astype(BF16), (k * jnp.exp(ek)).astype(BF16))
                a = jnp.where(mask, p, a)
                if lvl > 0:
                    m *= 2
            o = o + _dot(a.astype(BF16), v_bf)

            if d == 0:
                o_scr[pl.ds(r0, CHUNK), :] = o
            else:
                o_scr[pl.ds(r0, CHUNK), :] = o_scr[pl.ds(r0, CHUNK), :] + o

            kg = (k * jnp.exp(g_end - g)).astype(BF16)
            u_t = _dot(v.T.astype(BF16), kg)
            return jnp.exp(g_end) * st + u_t

        if has_state:
            st0 = s0_ref[0, d, 0].T
        else:
            st0 = jnp.zeros((DV_A, DK_A), F32)
        st = lax.fori_loop(0, n_chunks, chunk_body, st0)
        if want_state:
            sfin_ref[0, d, 0] = st.T

    o = o_scr[...]
    y = o * lax.rsqrt(jnp.mean(o * o, axis=-1, keepdims=True) + EPS) * g_ref[0]
    y_ref[0] = y * z_ref[0]


def _hgrn(p3, hgrn_g, s0, want_state):
    b, t, _ = p3.shape
    n_chunks = t // CHUNK
    has_state = s0 is not None

    def col(off):
        return pl.BlockSpec((1, t, 128), lambda bi, h: (bi, 0, off + h))

    in_specs = [col(QA), col(FFA), col(FBA), col(IA), col(ZA),
                pl.BlockSpec((1, 1, DV_A), lambda bi, h: (h, 0, 0))]
    args = [p3, p3, p3, p3, p3, hgrn_g.reshape(H_A, 1, DV_A)]
    st_spec = pl.BlockSpec((1, 2, 1, DK_A, DV_A), lambda bi, h: (bi, 0, h, 0, 0))
    if has_state:
        in_specs.append(st_spec)
        args.append(s0)
    out_specs = [pl.BlockSpec((1, t, 128), lambda bi, h: (bi, 0, h))]
    out_shape = [jax.ShapeDtypeStruct((b, t, H_A * DV_A), F32)]
    if want_state:
        out_specs.append(st_spec)
        out_shape.append(jax.ShapeDtypeStruct((b, 2, H_A, DK_A, DV_A), F32))
    res = pl.pallas_call(
        functools.partial(_hgrn_kernel, n_chunks=n_chunks, has_state=has_state,
                          want_state=want_state),
        grid=(b, H_A),
        in_specs=in_specs,
        out_specs=out_specs,
        out_shape=out_shape,
        scratch_shapes=[pltpu.VMEM((t, DV_A), F32)],
        compiler_params=_cparams(("arbitrary", "arbitrary")),
        name="hgrn2",
    )(*args)
    return res if want_state else (res[0], None)


def _rms64(x, g):
    lane = lax.broadcasted_iota(jnp.int32, x.shape, 1)
    lo = lane < DH_B
    x2 = x * x
    s0 = jnp.sum(jnp.where(lo, x2, 0.0), axis=-1, keepdims=True)
    s1 = jnp.sum(jnp.where(lo, 0.0, x2), axis=-1, keepdims=True)
    inv = jnp.where(lo, lax.rsqrt(s0 / DH_B + EPS), lax.rsqrt(s1 / DH_B + EPS))
    return x * inv * g


def _rope(x, cos, sin):
    lane = lax.broadcasted_iota(jnp.int32, x.shape, 1)
    first = (lane & 31) < 16
    partner = jnp.where(first, pltpu.roll(x, 112, 1), pltpu.roll(x, 16, 1))
    return x * cos + partner * sin


def _attn_kernel(*refs, t_self, t_ctx, tq, lam_init, latent, want_k):
    it = iter(refs)
    q_ref, k_ref, v_ref, z_ref = (next(it) for _ in range(4))
    qg_ref, kg_ref, sg_ref, lam_ref = (next(it) for _ in range(4))
    if latent:
        cq_ref, sq_ref, ck_ref, sk_ref, kc_ref, vc_ref = (next(it) for _ in range(6))
    y_ref = next(it)
    kout_ref = next(it) if want_k else None
    k_scr, v_scr = next(it), next(it)

    qi = pl.program_id(2)

    @pl.when(qi == 0)
    def _():
        kn = _rms64(k_ref[0], kg_ref[...])
        if want_k:
            kout_ref[0] = kn
        if latent:
            kn = _rope(kn, ck_ref[...], sk_ref[...])
            k_scr[t_self:t_self + t_ctx, :] = kc_ref[0, 0].astype(BF16)
            v_scr[t_self:t_self + t_ctx, :] = vc_ref[0, 0].astype(BF16)
        k_scr[0:t_self, :] = kn.astype(BF16)
        v_scr[0:t_self, :] = v_ref[0].astype(BF16)

    qn = _rms64(q_ref[0], qg_ref[...])
    if latent:
        qn = _rope(qn, cq_ref[...], sq_ref[...])
    qn = qn * (DH_B ** -0.5)
    lane = lax.broadcasted_iota(jnp.int32, qn.shape, 1)
    kf = k_scr[...]

    def probs(qm):
        s = _dot_nt(qm.astype(BF16), kf)
        e = jnp.exp(s - jnp.max(s, axis=-1, keepdims=True))
        return e / jnp.sum(e, axis=-1, keepdims=True)

    p0 = probs(jnp.where(lane < DH_B, qn, 0.0))
    p1 = probs(jnp.where(lane < DH_B, 0.0, qn))

    lp = lam_ref[...]
    l01 = jnp.sum(lp[0:1] * lp[1:2], axis=-1, keepdims=True)
    l23 = jnp.sum(lp[2:3] * lp[3:4], axis=-1, keepdims=True)
    lam = jnp.exp(l01) - jnp.exp(l23) + lam_init

    w = (p0 - lam * p1).astype(BF16)
    o = _dot(w, v_scr[...])
    y = o * lax.rsqrt(jnp.mean(o * o, axis=-1, keepdims=True) + EPS) * sg_ref[...]
    y_ref[0] = y * (1.0 - lam_init) * z_ref[0]


def _diff_attn(p3, layer, qn_g, kn_g, sub_g, lam_p, rope_tab, cache_k, cache_v, want_k):
    b, t, _ = p3.shape
    latent = rope_tab is not None
    tq = 256
    t_ctx = cache_k.shape[2] if latent else 0
    lam_init = 0.8 - 0.6 * math.exp(-0.3 * layer)

    def small(shape):
        return pl.BlockSpec(shape, lambda bi, h, qi: (0,) * len(shape))

    in_specs = [
        pl.BlockSpec((1, tq, 128), lambda bi, h, qi: (bi, qi, QB + h)),
        pl.BlockSpec((1, t, 128), lambda bi, h, qi: (bi, 0, KB + h)),
        pl.BlockSpec((1, t, 128), lambda bi, h, qi: (bi, 0, VB + h)),
        pl.BlockSpec((1, tq, 128), lambda bi, h, qi: (bi, qi, ZB + h)),
        small((1, 128)), small((1, 128)), small((1, 128)), small((4, DH_B)),
    ]
    args = [p3, p3, p3, p3, jnp.tile(qn_g, 2).reshape(1, 128), jnp.tile(kn_g, 2).reshape(1, 128),
            sub_g.reshape(1, 128), lam_p]
    if latent:
        cos_t, sin_t = rope_tab
        in_specs += [
            pl.BlockSpec((tq, 128), lambda bi, h, qi: (qi, 0)),
            pl.BlockSpec((tq, 128), lambda bi, h, qi: (qi, 0)),
            small((t, 128)), small((t, 128)),
            pl.BlockSpec((1, 1, t_ctx, 128), lambda bi, h, qi: (bi, layer, 0, h)),
            pl.BlockSpec((1, 1, t_ctx, 128), lambda bi, h, qi: (bi, layer, 0, h)),
        ]
        args += [cos_t, sin_t, cos_t, sin_t, cache_k, cache_v]
    out_specs = [pl.BlockSpec((1, tq, 128), lambda bi, h, qi: (bi, qi, h))]
    out_shape = [jax.ShapeDtypeStruct((b, t, H_B * DV_B), F32)]
    if want_k:
        out_specs.append(pl.BlockSpec((1, t, 128), lambda bi, h, qi: (bi, 0, h)))
        out_shape.append(jax.ShapeDtypeStruct((b, t, H_B * 2 * DH_B), F32))
    res = pl.pallas_call(
        functools.partial(_attn_kernel, t_self=t, t_ctx=t_ctx, tq=tq, lam_init=lam_init,
                          latent=latent, want_k=want_k),
        grid=(b, H_B, t // tq),
        in_specs=in_specs,
        out_specs=out_specs,
        out_shape=out_shape,
        scratch_shapes=[pltpu.VMEM((t + t_ctx, 128), BF16), pltpu.VMEM((t + t_ctx, 128), BF16)],
        compiler_params=_cparams(("arbitrary", "arbitrary", "arbitrary")),
        name="diff_attn",
    )(*args)
    return res if want_k else (res[0], None)


def _dft_tables(t):
    def cs(n):
        idx = np.arange(n, dtype=np.int64)
        ang = 2.0 * np.pi * ((idx[:, None] * idx[None, :]) % n).astype(np.float64) / n
        return np.cos(ang), np.sin(ang)

    cc, sc = cs(DG_C)
    ct, st = cs(t)
    chan = jnp.asarray(np.concatenate([cc, sc], axis=1), dtype=F32)
    pos = jnp.asarray(np.concatenate([ct, -st], axis=1), dtype=F32)
    return chan.astype(BF16), pos.astype(BF16)


def _fourier_kernel(u_ref, z_ref, chan_ref, pos_ref, y_ref, uc_scr, *, t, scale):
    ri = pl.program_id(2)

    @pl.when(ri == 0)
    def _():
        u1 = _dot(u_ref[0].astype(BF16), chan_ref[...])
        uc_scr[0:t, :] = u1[:, 0:DG_C].astype(BF16)
        uc_scr[t:2 * t, :] = u1[:, DG_C:2 * DG_C].astype(BF16)

    f = _dot(pos_ref[...], uc_scr[...])
    y_ref[0] = f * scale * z_ref[0]


def _fourier(p3):
    b, t, _ = p3.shape
    tr = min(t, 512)
    chan, pos = _dft_tables(t)
    scale = 1.0 / math.sqrt(t * DG_C)
    return pl.pallas_call(
        functools.partial(_fourier_kernel, t=t, scale=scale),
        grid=(b, G_C, t // tr),
        in_specs=[
            pl.BlockSpec((1, t, 128), lambda bi, g, ri: (bi, 0, UC + g)),
            pl.BlockSpec((1, tr, 128), lambda bi, g, ri: (bi, ri, ZC + g)),
            pl.BlockSpec((DG_C, 2 * DG_C), lambda bi, g, ri: (0, 0)),
            pl.BlockSpec((tr, 2 * t), lambda bi, g, ri: (ri, 0)),
        ],
        out_specs=pl.BlockSpec((1, tr, 128), lambda bi, g, ri: (bi, ri, g)),
        out_shape=jax.ShapeDtypeStruct((b, t, G_C * DG_C), F32),
        scratch_shapes=[pltpu.VMEM((2 * t, DG_C), BF16)],
        compiler_params=_cparams(("arbitrary", "arbitrary", "arbitrary")),
        name="fourier_mix",
    )(p3, p3, chan, pos)


def _merge_kernel(x_ref, mod_ref, ya_ref, yb_ref, yc_ref, *rest):
    g_refs = rest[0:6]
    wa_ref, wb_ref, wc_ref, wo_ref, o_ref = rest[6:11]
    half = SEG
    parts = []
    for n in range(2):
        cols = slice(n * half, (n + 1) * half)
        m = g_refs[0 + n][...] * _dot(ya_ref[...].astype(BF16), wa_ref[:, cols])
        m = m + g_refs[2 + n][...] * _dot(yb_ref[...].astype(BF16), wb_ref[:, cols])
        m = m + g_refs[4 + n][...] * _dot(yc_ref[...].astype(BF16), wc_ref[:, cols])
        parts.append(m.astype(BF16))
    merged = jnp.concatenate(parts, axis=1)
    gate = mod_ref[0, 2:3, :]
    o_ref[...] = x_ref[...] + gate * _dot(merged, wo_ref[...])


def _merge(x2, mod3, row_of_tile, p2, ya, yb, yc, wa, wb, wc, wo, tm):
    n = x2.shape[0]
    tok = lambda i: (i, 0)
    const = lambda i: (0, 0)
    in_specs = [
        pl.BlockSpec((tm, D_MODEL), tok),
        pl.BlockSpec((1, 3, D_MODEL), lambda i: (row_of_tile(i), 0, 0)),
        pl.BlockSpec((tm, SEG), tok), pl.BlockSpec((tm, SEG), tok), pl.BlockSpec((tm, SEG), tok),
    ]
    in_specs += [pl.BlockSpec((tm, SEG), (lambda i, s=s: (i, GATE_SEG + s))) for s in range(6)]
    in_specs += [pl.BlockSpec((SEG, D_MODEL), const)] * 3 + [pl.BlockSpec((D_MODEL, D_MODEL), const)]
    return pl.pallas_call(
        _merge_kernel,
        grid=(n // tm,),
        in_specs=in_specs,
        out_specs=pl.BlockSpec((tm, D_MODEL), tok),
        out_shape=jax.ShapeDtypeStruct((n, D_MODEL), F32),
        compiler_params=_cparams(("arbitrary",)),
        name="merge_out",
    )(x2, mod3, ya, yb, yc, p2, p2, p2, p2, p2, p2, wa, wb, wc, wo)


def _rope_tables(t):
    nf = DH_B // 4
    n_rows = t // GRID_W
    pos_row = jnp.repeat(jnp.arange(n_rows, dtype=F32), GRID_W)
    pos_col = jnp.tile(jnp.arange(GRID_W, dtype=F32), n_rows)
    inv = ROPE_BASE ** (-jnp.arange(nf, dtype=F32) / nf)
    ar = pos_row[:, None] * inv[None, :]
    ac = pos_col[:, None] * inv[None, :]
    cos64 = jnp.concatenate([jnp.cos(ar), jnp.cos(ar), jnp.cos(ac), jnp.cos(ac)], axis=1)
    sin64 = jnp.concatenate([-jnp.sin(ar), jnp.sin(ar), -jnp.sin(ac), jnp.sin(ac)], axis=1)
    return jnp.tile(cos64, (1, 2)), jnp.tile(sin64, (1, 2))


def _layer(x, mod3, row_of_tile, layer, lw, log_lb, log1m_lb, tm, ctx=None, rope_tab=None):
    (norm_g, w_in_bf, hgrn_g, qn_g, kn_g, lam_p, sub_g, wa, wb, wc, wo) = lw
    b, t, _ = x.shape
    x2 = x.reshape(b * t, D_MODEL)
    p2 = _in_proj(x2, mod3, row_of_tile, norm_g, w_in_bf, log_lb, log1m_lb, tm)
    p3 = p2.reshape(b, t, D_IN)
    is_ctx = ctx is None
    s0 = None if is_ctx else ctx[2]
    ya, s_fin = _hgrn(p3, hgrn_g, s0, want_state=is_ctx)
    yb, k_new = _diff_attn(p3, layer, qn_g, kn_g, sub_g, lam_p, rope_tab,
                           None if is_ctx else ctx[0], None if is_ctx else ctx[1],
                           want_k=is_ctx)
    yc = _fourier(p3)
    x_new = _merge(x2, mod3, row_of_tile, p2, ya.reshape(b * t, -1), yb.reshape(b * t, -1),
                   yc.reshape(b * t, -1), wa, wb, wc, wo, tm)
    x_new = x_new.reshape(b, t, D_MODEL)
    if is_ctx:
        v_new = p3[:, :, VB * 128:VB * 128 + H_B * DV_B]
        return x_new, k_new, v_new, s_fin
    return x_new


def kernel(x_prompt, x_sample, c, cache_diff_k, cache_diff_v, state_hgrn, c_ctx, ada_w, ada_b,
           norm_g, w_in, hgrn_lb, hgrn_norm_g, diff_qn_g, diff_kn_g, diff_lambda, diff_subln_g,
           w_branch_a, w_branch_b, w_branch_c, w_out):
    batch, seq, _ = x_prompt.shape
    dec_b, dec_t, _ = x_sample.shape
    past = cache_diff_k.shape[2]

    lb_all = jnp.cumsum(jax.nn.softmax(hgrn_lb.astype(F32), axis=0), axis=0)
    lb_all = lb_all - lb_all[0]
    log_lb = jnp.log(lb_all)
    log1m_lb = jnp.log1p(-lb_all)

    cond8 = jnp.zeros((8, D_MODEL), F32).at[0].set(c_ctx).at[1:1 + dec_b].set(c)
    mod = _modulation(cond8, ada_w, ada_b).reshape(DEPTH, 8, 3, D_MODEL)

    tm = 512
    rope_tab = _rope_tables(dec_t)
    cache_k4 = cache_diff_k.reshape(dec_b, DEPTH, past, H_B * 2 * DH_B)
    cache_v4 = cache_diff_v.reshape(dec_b, DEPTH, past, H_B * DV_B)
    tiles_per_seq = dec_t // tm
    ctx_row = lambda i: 0
    lat_row = lambda i: 1 + i // tiles_per_seq

    xp, xs = x_prompt, x_sample
    new_k, new_v, new_s = [], [], []
    for l in range(DEPTH):
        lw = (norm_g[l], w_in[l].astype(BF16), hgrn_norm_g[l], diff_qn_g[l], diff_kn_g[l],
              diff_lambda[l], diff_subln_g[l], w_branch_a[l].astype(BF16),
              w_branch_b[l].astype(BF16), w_branch_c[l].astype(BF16), w_out[l].astype(BF16))
        mod3 = mod[l]
        xp, k_c, v_c, s_c = _layer(xp, mod3, ctx_row, l, lw, log_lb[l], log1m_lb[l], tm)
        new_k.append(k_c)
        new_v.append(v_c)
        new_s.append(s_c)
        xs = _layer(xs, mod3, lat_row, l, lw, log_lb[l], log1m_lb[l], tm,
                    ctx=(cache_k4, cache_v4, state_hgrn[:, l]), rope_tab=rope_tab)
    new_k = jnp.stack(new_k, axis=1).reshape(batch, DEPTH, seq, H_B, 2, DH_B)
    new_v = jnp.stack(new_v, axis=1).reshape(batch, DEPTH, seq, H_B, DV_B)
    new_s = jnp.stack(new_s, axis=1)
    return (xp, xs, new_k, new_v, new_s)
```

```python
import functools
import math

import numpy as np
import jax
import jax.numpy as jnp
from jax import lax
from jax.experimental import pallas as pl
from jax.experimental.pallas import tpu as pltpu

F32 = jnp.float32
BF16 = jnp.bfloat16

D_MODEL = 1024
DEPTH = 2
GRID_W = 64
EPS = 1e-6
H_A = 4
DK_A = 128
DV_A = 128
H_B = 4
DH_B = 64
DV_B = 128
G_C = 4
DG_C = 128
ROPE_BASE = 10000.0
SEG = 512
(S_QA, S_FFA, S_FBA, S_IA, S_ZA, S_QB, S_KB, S_VB, S_ZB, S_UC, S_ZC) = range(11)
S_GATE = 11
PROJ_SEGS = (S_QA, S_FFA, S_FBA, S_IA, S_QB, S_KB, S_VB, S_UC)
MERGE_SEGS = (S_ZA, S_ZB, S_ZC) + tuple(range(S_GATE, S_GATE + 6))

CHUNK = 128
SUB = 16
HEADS_PER_STEP = 2
TOKEN_TILE = 512
VMEM_LIMIT = 56 * 1024 * 1024


def _cparams(sem):
    return pltpu.CompilerParams(dimension_semantics=sem, vmem_limit_bytes=VMEM_LIMIT)


def _resident(shape):
    return pl.BlockSpec(shape, lambda *_: (0,) * len(shape), pipeline_mode=pl.Buffered(1))


def _silu(x):
    return x * jax.nn.sigmoid(x)


def _log_sigmoid(x):
    return jnp.minimum(x, 0.0) - jnp.log1p(jnp.exp(-jnp.abs(x)))


def _dot(a, b):
    return jnp.dot(a, b, preferred_element_type=F32)


def _dot_nt(a, b):
    return lax.dot_general(a, b, (((1,), (1,)), ((), ())), preferred_element_type=F32)


def _modulated_norm(x, g, mod_ref):
    y = x * lax.rsqrt(jnp.mean(x * x, axis=-1, keepdims=True) + EPS) * g
    shift = mod_ref[0, 0:1, :]
    scale = mod_ref[0, 1:2, :]
    return (y * (1.0 + scale) + shift).astype(BF16)


def _mod_kernel(c_ref, w_ref, b_ref, o_ref):
    c = c_ref[...]
    s = _silu(c).astype(BF16)
    o_ref[0] = _dot(s, w_ref[0].astype(BF16)) + b_ref[0]


def _modulation(cond8, ada_w, ada_b):
    tn = 512
    return pl.pallas_call(
        _mod_kernel,
        grid=(DEPTH, 3 * D_MODEL // tn),
        in_specs=[
            pl.BlockSpec((8, D_MODEL), lambda l, j: (0, 0)),
            pl.BlockSpec((1, D_MODEL, tn), lambda l, j: (l, 0, j)),
            pl.BlockSpec((1, 1, tn), lambda l, j: (l, 0, j)),
        ],
        out_specs=pl.BlockSpec((1, 8, tn), lambda l, j: (l, 0, j)),
        out_shape=jax.ShapeDtypeStruct((DEPTH, 8, 3 * D_MODEL), F32),
        compiler_params=_cparams(("arbitrary", "arbitrary")),
        name="adaln_mod",
    )(cond8, ada_w, ada_b.reshape(DEPTH, 1, 3 * D_MODEL))


def _proj_kernel(x_ref, mod_ref, g_ref, w_ref, la_ref, l1_ref, pa_ref, pb_ref, pc_ref):
    h = _modulated_norm(x_ref[...], g_ref[...], mod_ref)

    def seg(s):
        return _dot(h, w_ref[:, s * SEG:(s + 1) * SEG])

    def log_f(raw, d):
        a = la_ref[d:d + 1, :]
        b = l1_ref[d:d + 1, :] + _log_sigmoid(raw)
        return jnp.maximum(a, b) + jnp.log1p(jnp.exp(-jnp.abs(a - b)))

    pa_ref[:, 0 * SEG:1 * SEG] = _silu(seg(0))
    pa_ref[:, 1 * SEG:2 * SEG] = log_f(seg(1), 0)
    pa_ref[:, 2 * SEG:3 * SEG] = log_f(seg(2), 1)
    pa_ref[:, 3 * SEG:4 * SEG] = seg(3)
    for s in range(3):
        pb_ref[:, s * SEG:(s + 1) * SEG] = seg(4 + s)
    pc_ref[...] = seg(7)


def _in_proj(x2, mod3, row_of_tile, norm_g, w_proj, log_lb, log1m_lb):
    n = x2.shape[0]
    tm = TOKEN_TILE
    tok = lambda i: (i, 0)
    return pl.pallas_call(
        _proj_kernel,
        grid=(n // tm,),
        in_specs=[
            pl.BlockSpec((tm, D_MODEL), tok),
            pl.BlockSpec((1, 3, D_MODEL), lambda i: (row_of_tile(i), 0, 0)),
            _resident((1, D_MODEL)),
            _resident((D_MODEL, len(PROJ_SEGS) * SEG)),
            _resident((2, SEG)),
            _resident((2, SEG)),
        ],
        out_specs=[pl.BlockSpec((tm, 4 * SEG), tok), pl.BlockSpec((tm, 3 * SEG), tok),
                   pl.BlockSpec((tm, SEG), tok)],
        out_shape=[jax.ShapeDtypeStruct((n, 4 * SEG), F32), jax.ShapeDtypeStruct((n, 3 * SEG), F32),
                   jax.ShapeDtypeStruct((n, SEG), F32)],
        compiler_params=_cparams(("arbitrary",)),
        name="in_proj",
    )(x2, mod3, norm_g.reshape(1, D_MODEL), w_proj, log_lb, log1m_lb)


def _split3(x):
    hi = x.astype(BF16)
    r = x - hi.astype(F32)
    mid = r.astype(BF16)
    lo = (r - mid.astype(F32)).astype(BF16)
    return hi, mid, lo


def _level_exponents(g, m, reverse, diag):
    nb = CHUNK // m
    eq, ek = [], []
    ninf = jnp.full((m, 128), -jnp.inf, F32)
    for i in range(nb):
        sl = g[i * m:(i + 1) * m]
        if diag:
            r = i * m + (m // 2 - 1 if not reverse else m // 2)
            ref = g[r:r + 1]
            eq.append(sl - ref)
            ek.append(ref - sl)
        elif not reverse:
            if i % 2 == 1:
                eq.append(sl - g[i * m - 1:i * m])
                ek.append(ninf)
            else:
                eq.append(ninf)
                ek.append(g[(i + 1) * m - 1:(i + 1) * m] - sl)
        else:
            if i % 2 == 0:
                eq.append(sl - g[(i + 1) * m:(i + 1) * m + 1])
                ek.append(ninf)
            else:
                eq.append(ninf)
                ek.append(g[i * m:i * m + 1] - sl)
    return jnp.concatenate(eq, axis=0), jnp.concatenate(ek, axis=0)


def _level_masks(reverse):
    t = lax.broadcasted_iota(jnp.int32, (CHUNK, CHUNK), 0)
    s = lax.broadcasted_iota(jnp.int32, (CHUNK, CHUNK), 1)
    sh = int(math.log2(SUB))
    same = (t >> sh) == (s >> sh)
    masks = [same & ((s >= t) if reverse else (s <= t))]
    m = SUB
    while m < CHUNK:
        sh = int(math.log2(m))
        tb, sb = t >> sh, s >> sh
        if not reverse:
            masks.append(((tb & 1) == 1) & (sb == tb - 1))
        else:
            masks.append(((tb & 1) == 0) & (sb == tb + 1))
        m *= 2
    return masks


def _hgrn_chunk(lf, q, v, st, tri, masks, reverse):
    end_row = 0 if reverse else CHUNK - 1
    k = 1.0 - jnp.exp(lf)
    hi, mid, lo = _split3(lf)
    g = _dot(tri, hi) + _dot(tri, mid) + _dot(tri, lo)
    g_end = g[end_row:end_row + 1]
    v_bf = v.astype(BF16)

    qg = (q * jnp.exp(g)).astype(BF16)
    o = _dot_nt(qg, st.astype(BF16))

    a = jnp.zeros((CHUNK, CHUNK), F32)
    m = SUB
    for lvl, mask in enumerate(masks):
        eq, ek = _level_exponents(g, m, reverse, diag=(lvl == 0))
        p = _dot_nt((q * jnp.exp(eq)).astype(BF16), (k * jnp.exp(ek)).astype(BF16))
        a = jnp.where(mask, p, a)
        if lvl > 0:
            m *= 2
    o = o + _dot(a.astype(BF16), v_bf)

    kg = (k * jnp.exp(g_end - g)).astype(BF16)
    u_t = _dot(v.T.astype(BF16), kg)
    return o, jnp.exp(g_end) * st + u_t


def _hgrn_kernel(*refs, n_chunks, has_state, want_state):
    it = iter(refs)
    q_ref, lff_ref, lfb_ref, v_ref, g_ref = (next(it) for _ in range(5))
    s0_ref = next(it) if has_state else None
    y_ref = next(it)
    sfin_ref = next(it) if want_state else None
    o_scr, st_scr = next(it), next(it)
    hpb = HEADS_PER_STEP

    t_i = lax.broadcasted_iota(jnp.int32, (CHUNK, CHUNK), 0)
    s_i = lax.broadcasted_iota(jnp.int32, (CHUNK, CHUNK), 1)
    tris = [jnp.where(s_i <= t_i, 1.0, 0.0).astype(BF16), jnp.where(s_i >= t_i, 1.0, 0.0).astype(BF16)]
    masks = [_level_masks(False), _level_masks(True)]
    lf_refs = (lff_ref, lfb_ref)

    for hh in range(hpb):
        for d in range(2):
            if has_state:
                st_scr[2 * hh + d] = s0_ref[0, d, hh].T
            else:
                st_scr[2 * hh + d] = jnp.zeros((DV_A, DK_A), F32)

    def step(ci):
        for hh in range(hpb):
            ls = slice(hh * 128, (hh + 1) * 128)
            for d in range(2):
                c = ci if d == 0 else n_chunks - 1 - ci
                r0 = c * CHUNK
                if not isinstance(r0, int):
                    r0 = pl.multiple_of(r0, CHUNK)
                rows = pl.ds(r0, CHUNK)
                o, st = _hgrn_chunk(lf_refs[d][0, rows, ls], q_ref[0, rows, ls], v_ref[0, rows, ls],
                                    st_scr[2 * hh + d], tris[d], masks[d], reverse=(d == 1))
                o_scr[d, rows, ls] = o
                st_scr[2 * hh + d] = st

    if n_chunks <= 2:
        for ci in range(n_chunks):
            step(ci)
    else:
        def body(i, carry):
            step(2 * i)
            step(2 * i + 1)
            return carry
        lax.fori_loop(0, n_chunks // 2, body, 0)

    for hh in range(hpb):
        ls = slice(hh * 128, (hh + 1) * 128)
        if want_state:
            for d in range(2):
                sfin_ref[0, d, hh] = st_scr[2 * hh + d].T
        o = o_scr[0, :, ls] + o_scr[1, :, ls]
        y_ref[0, :, ls] = o * lax.rsqrt(jnp.mean(o * o, axis=-1, keepdims=True) + EPS) * g_ref[0, :, ls]


def _hgrn(pa3, hgrn_g, s0, want_state):
    b, t, _ = pa3.shape
    n_chunks = t // CHUNK
    has_state = s0 is not None
    hpb = HEADS_PER_STEP
    w = 128 * hpb
    per_seg = SEG // w

    def col(seg):
        return pl.BlockSpec((1, t, w), lambda bi, hp: (bi, 0, seg * per_seg + hp))

    in_specs = [col(0), col(1), col(2), col(3),
                pl.BlockSpec((1, 1, w), lambda bi, hp: (hp, 0, 0))]
    args = [pa3, pa3, pa3, pa3, hgrn_g.reshape(H_A // hpb, 1, w)]
    st_spec = pl.BlockSpec((1, 2, hpb, DK_A, DV_A), lambda bi, hp: (bi, 0, hp, 0, 0))
    if has_state:
        in_specs.append(st_spec)
        args.append(s0)
    out_specs = [pl.BlockSpec((1, t, w), lambda bi, hp: (bi, 0, hp))]
    out_shape = [jax.ShapeDtypeStruct((b, t, H_A * DV_A), F32)]
    if want_state:
        out_specs.append(st_spec)
        out_shape.append(jax.ShapeDtypeStruct((b, 2, H_A, DK_A, DV_A), F32))
    res = pl.pallas_call(
        functools.partial(_hgrn_kernel, n_chunks=n_chunks, has_state=has_state,
                          want_state=want_state),
        grid=(b, H_A // hpb),
        in_specs=in_specs,
        out_specs=out_specs,
        out_shape=out_shape,
        scratch_shapes=[pltpu.VMEM((2, t, w), F32), pltpu.VMEM((2 * hpb, DV_A, DK_A), F32)],
        compiler_params=_cparams(("arbitrary", "arbitrary")),
        name="hgrn2",
    )(*args)
    return res if want_state else (res[0], None)


def _rms64(x, g):
    lane = lax.broadcasted_iota(jnp.int32, x.shape, 1)
    lo = lane < DH_B
    x2 = x * x
    s0 = jnp.sum(jnp.where(lo, x2, 0.0), axis=-1, keepdims=True)
    s1 = jnp.sum(jnp.where(lo, 0.0, x2), axis=-1, keepdims=True)
    inv = jnp.where(lo, lax.rsqrt(s0 / DH_B + EPS), lax.rsqrt(s1 / DH_B + EPS))
    return x * inv * g


def _rope(x, cos, sin):
    lane = lax.broadcasted_iota(jnp.int32, x.shape, 1)
    first = (lane & 31) < 16
    partner = jnp.where(first, pltpu.roll(x, 112, 1), pltpu.roll(x, 16, 1))
    return x * cos + partner * sin


def _attn_kernel(*refs, t_self, t_ctx, lam_init, latent, want_k):
    it = iter(refs)
    q_ref, k_ref, v_ref = (next(it) for _ in range(3))
    qg_ref, kg_ref, sg_ref, lam_ref = (next(it) for _ in range(4))
    if latent:
        cq_ref, sq_ref, ck_ref, sk_ref, kc_ref, vc_ref = (next(it) for _ in range(6))
    y_ref = next(it)
    kout_ref = next(it) if want_k else None
    k_scr, v_scr = next(it), next(it)

    qi = pl.program_id(2)

    @pl.when(qi == 0)
    def _():
        kn = _rms64(k_ref[0], kg_ref[...])
        if want_k:
            kout_ref[0] = kn
        if latent:
            kn = _rope(kn, ck_ref[...], sk_ref[...])
            k_scr[t_self:t_self + t_ctx, :] = kc_ref[0, 0].astype(BF16)
            v_scr[t_self:t_self + t_ctx, :] = vc_ref[0, 0].astype(BF16)
        k_scr[0:t_self, :] = kn.astype(BF16)
        v_scr[0:t_self, :] = v_ref[0].astype(BF16)

    qn = _rms64(q_ref[0], qg_ref[...])
    if latent:
        qn = _rope(qn, cq_ref[...], sq_ref[...])
    qn = qn * (DH_B ** -0.5)
    lane = lax.broadcasted_iota(jnp.int32, qn.shape, 1)
    kf = k_scr[...]

    def probs(qm):
        s = _dot_nt(qm.astype(BF16), kf)
        e = jnp.exp(s - jnp.max(s, axis=-1, keepdims=True))
        return e / jnp.sum(e, axis=-1, keepdims=True)

    p0 = probs(jnp.where(lane < DH_B, qn, 0.0))
    p1 = probs(jnp.where(lane < DH_B, 0.0, qn))

    lp = lam_ref[...]
    l01 = jnp.sum(lp[0:1] * lp[1:2], axis=-1, keepdims=True)
    l23 = jnp.sum(lp[2:3] * lp[3:4], axis=-1, keepdims=True)
    lam = jnp.exp(l01) - jnp.exp(l23) + lam_init

    w = (p0 - lam * p1).astype(BF16)
    o = _dot(w, v_scr[...])
    y = o * lax.rsqrt(jnp.mean(o * o, axis=-1, keepdims=True) + EPS) * sg_ref[...]
    y_ref[0] = y * (1.0 - lam_init)


def _diff_attn(pb3, layer, qn_g, kn_g, sub_g, lam_p, rope_tab, cache_k, cache_v, want_k):
    b, t, _ = pb3.shape
    latent = rope_tab is not None
    tq = 256
    t_ctx = cache_k.shape[2] if latent else 0
    lam_init = 0.8 - 0.6 * math.exp(-0.3 * layer)

    def small(shape):
        return pl.BlockSpec(shape, lambda bi, h, qi: (0,) * len(shape))

    in_specs = [
        pl.BlockSpec((1, tq, 128), lambda bi, h, qi: (bi, qi, h)),
        pl.BlockSpec((1, t, 128), lambda bi, h, qi: (bi, 0, H_B + h)),
        pl.BlockSpec((1, t, 128), lambda bi, h, qi: (bi, 0, 2 * H_B + h)),
        small((1, 128)), small((1, 128)), small((1, 128)), small((4, DH_B)),
    ]
    args = [pb3, pb3, pb3, jnp.tile(qn_g, 2).reshape(1, 128), jnp.tile(kn_g, 2).reshape(1, 128),
            sub_g.reshape(1, 128), lam_p]
    if latent:
        cos_t, sin_t = rope_tab
        in_specs += [
            pl.BlockSpec((tq, 128), lambda bi, h, qi: (qi, 0)),
            pl.BlockSpec((tq, 128), lambda bi, h, qi: (qi, 0)),
            small((t, 128)), small((t, 128)),
            pl.BlockSpec((1, 1, t_ctx, 128), lambda bi, h, qi: (bi, layer, 0, h)),
            pl.BlockSpec((1, 1, t_ctx, 128), lambda bi, h, qi: (bi, layer, 0, h)),
        ]
        args += [cos_t, sin_t, cos_t, sin_t, cache_k, cache_v]
    out_specs = [pl.BlockSpec((1, tq, 128), lambda bi, h, qi: (bi, qi, h))]
    out_shape = [jax.ShapeDtypeStruct((b, t, H_B * DV_B), F32)]
    if want_k:
        out_specs.append(pl.BlockSpec((1, t, 128), lambda bi, h, qi: (bi, 0, h)))
        out_shape.append(jax.ShapeDtypeStruct((b, t, H_B * 2 * DH_B), F32))
    res = pl.pallas_call(
        functools.partial(_attn_kernel, t_self=t, t_ctx=t_ctx, lam_init=lam_init,
                          latent=latent, want_k=want_k),
        grid=(b, H_B, t // tq),
        in_specs=in_specs,
        out_specs=out_specs,
        out_shape=out_shape,
        scratch_shapes=[pltpu.VMEM((t + t_ctx, 128), BF16), pltpu.VMEM((t + t_ctx, 128), BF16)],
        compiler_params=_cparams(("arbitrary", "arbitrary", "arbitrary")),
        name="diff_attn",
    )(*args)
    return res if want_k else (res[0], None)


def _dft_tables(t):
    def cs(n):
        idx = np.arange(n, dtype=np.int64)
        ang = 2.0 * np.pi * ((idx[:, None] * idx[None, :]) % n).astype(np.float64) / n
        return np.cos(ang), np.sin(ang)

    cc, sc = cs(DG_C)
    ct, st = cs(t)
    chan = jnp.asarray(np.concatenate([cc, sc], axis=1), dtype=F32)
    pos = jnp.asarray(np.concatenate([ct, -st], axis=1), dtype=F32)
    return chan.astype(BF16), pos.astype(BF16)


def _fourier_kernel(u_ref, chan_ref, pos_ref, y_ref, uc_scr, *, t, scale):
    ri = pl.program_id(2)

    @pl.when(ri == 0)
    def _():
        u1 = _dot(u_ref[0].astype(BF16), chan_ref[...])
        uc_scr[0:t, :] = u1[:, 0:DG_C].astype(BF16)
        uc_scr[t:2 * t, :] = u1[:, DG_C:2 * DG_C].astype(BF16)

    y_ref[0] = _dot(pos_ref[...], uc_scr[...]) * scale


def _fourier(pc3):
    b, t, _ = pc3.shape
    tr = min(t, 512)
    chan, pos = _dft_tables(t)
    scale = 1.0 / math.sqrt(t * DG_C)
    return pl.pallas_call(
        functools.partial(_fourier_kernel, t=t, scale=scale),
        grid=(b, G_C, t // tr),
        in_specs=[
            pl.BlockSpec((1, t, 128), lambda bi, g, ri: (bi, 0, g)),
            pl.BlockSpec((DG_C, 2 * DG_C), lambda bi, g, ri: (0, 0)),
            pl.BlockSpec((tr, 2 * t), lambda bi, g, ri: (ri, 0)),
        ],
        out_specs=pl.BlockSpec((1, tr, 128), lambda bi, g, ri: (bi, ri, g)),
        out_shape=jax.ShapeDtypeStruct((b, t, G_C * DG_C), F32),
        scratch_shapes=[pltpu.VMEM((2 * t, DG_C), BF16)],
        compiler_params=_cparams(("arbitrary", "arbitrary", "arbitrary")),
        name="fourier_mix",
    )(pc3, chan, pos)


def _merge_kernel(x_ref, mod_ref, g_ref, oa_ref, ob_ref, oc_ref, wm_ref, wa_ref, wb_ref, wc_ref,
                  wo_ref, o_ref):
    x = x_ref[...]
    h = _modulated_norm(x, g_ref[...], mod_ref)

    def seg(s):
        return _dot(h, wm_ref[:, s * SEG:(s + 1) * SEG])

    ya = (oa_ref[...] * _silu(seg(0))).astype(BF16)
    yb = (ob_ref[...] * _silu(seg(1))).astype(BF16)
    yc = (oc_ref[...] * _silu(seg(2))).astype(BF16)
    parts = []
    for n in range(2):
        cols = slice(n * SEG, (n + 1) * SEG)
        m = jax.nn.sigmoid(seg(3 + n)) * _dot(ya, wa_ref[:, cols])
        m = m + jax.nn.sigmoid(seg(5 + n)) * _dot(yb, wb_ref[:, cols])
        m = m + jax.nn.sigmoid(seg(7 + n)) * _dot(yc, wc_ref[:, cols])
        parts.append(m.astype(BF16))
    merged = jnp.concatenate(parts, axis=1)
    gate = mod_ref[0, 2:3, :]
    o_ref[...] = x + gate * _dot(merged, wo_ref[...])


def _merge(x2, mod3, row_of_tile, norm_g, oa, ob, oc, w_merge, wa, wb, wc, wo):
    n = x2.shape[0]
    tm = TOKEN_TILE
    tok = lambda i: (i, 0)
    in_specs = [
        pl.BlockSpec((tm, D_MODEL), tok),
        pl.BlockSpec((1, 3, D_MODEL), lambda i: (row_of_tile(i), 0, 0)),
        _resident((1, D_MODEL)),
        pl.BlockSpec((tm, SEG), tok), pl.BlockSpec((tm, SEG), tok), pl.BlockSpec((tm, SEG), tok),
        _resident((D_MODEL, len(MERGE_SEGS) * SEG)),
        _resident((SEG, D_MODEL)), _resident((SEG, D_MODEL)), _resident((SEG, D_MODEL)),
        _resident((D_MODEL, D_MODEL)),
    ]
    return pl.pallas_call(
        _merge_kernel,
        grid=(n // tm,),
        in_specs=in_specs,
        out_specs=pl.BlockSpec((tm, D_MODEL), tok),
        out_shape=jax.ShapeDtypeStruct((n, D_MODEL), F32),
        compiler_params=_cparams(("arbitrary",)),
        name="merge_out",
    )(x2, mod3, norm_g.reshape(1, D_MODEL), oa, ob, oc, w_merge, wa, wb, wc, wo)


def _rope_tables(t):
    nf = DH_B // 4
    n_rows = t // GRID_W
    pos_row = jnp.repeat(jnp.arange(n_rows, dtype=F32), GRID_W)
    pos_col = jnp.tile(jnp.arange(GRID_W, dtype=F32), n_rows)
    inv = ROPE_BASE ** (-jnp.arange(nf, dtype=F32) / nf)
    ar = pos_row[:, None] * inv[None, :]
    ac = pos_col[:, None] * inv[None, :]
    cos64 = jnp.concatenate([jnp.cos(ar), jnp.cos(ar), jnp.cos(ac), jnp.cos(ac)], axis=1)
    sin64 = jnp.concatenate([-jnp.sin(ar), jnp.sin(ar), -jnp.sin(ac), jnp.sin(ac)], axis=1)
    return jnp.tile(cos64, (1, 2)), jnp.tile(sin64, (1, 2))


def _take_segments(w, segs):
    return jnp.concatenate([w[:, s * SEG:(s + 1) * SEG] for s in segs], axis=1).astype(BF16)


def _layer(x, mod3, row_of_tile, layer, lw, log_lb, log1m_lb, ctx=None, rope_tab=None):
    (norm_g, w_proj, w_merge, hgrn_g, qn_g, kn_g, lam_p, sub_g, wa, wb, wc, wo) = lw
    b, t, _ = x.shape
    x2 = x.reshape(b * t, D_MODEL)
    pa, pb, pc = _in_proj(x2, mod3, row_of_tile, norm_g, w_proj, log_lb, log1m_lb)
    pb3 = pb.reshape(b, t, 3 * SEG)
    is_ctx = ctx is None
    s0 = None if is_ctx else ctx[2]
    oa, s_fin = _hgrn(pa.reshape(b, t, 4 * SEG), hgrn_g, s0, want_state=is_ctx)
    ob, k_new = _diff_attn(pb3, layer, qn_g, kn_g, sub_g, lam_p, rope_tab,
                           None if is_ctx else ctx[0], None if is_ctx else ctx[1],
                           want_k=is_ctx)
    oc = _fourier(pc.reshape(b, t, SEG))
    x_new = _merge(x2, mod3, row_of_tile, norm_g, oa.reshape(b * t, SEG), ob.reshape(b * t, SEG),
                   oc.reshape(b * t, SEG), w_merge, wa, wb, wc, wo)
    x_new = x_new.reshape(b, t, D_MODEL)
    if is_ctx:
        return x_new, k_new, pb3[:, :, 2 * SEG:3 * SEG], s_fin
    return x_new


def kernel(x_prompt, x_sample, c, cache_diff_k, cache_diff_v, state_hgrn, c_ctx, ada_w, ada_b,
           norm_g, w_in, hgrn_lb, hgrn_norm_g, diff_qn_g, diff_kn_g, diff_lambda, diff_subln_g,
           w_branch_a, w_branch_b, w_branch_c, w_out):
    batch, seq, _ = x_prompt.shape
    dec_b, dec_t, _ = x_sample.shape
    past = cache_diff_k.shape[2]

    lb_all = jnp.cumsum(jax.nn.softmax(hgrn_lb.astype(F32), axis=0), axis=0)
    lb_all = lb_all - lb_all[0]
    log_lb = jnp.log(lb_all)
    log1m_lb = jnp.log1p(-lb_all)

    cond8 = jnp.zeros((8, D_MODEL), F32).at[0].set(c_ctx).at[1:1 + dec_b].set(c)
    mod = _modulation(cond8, ada_w, ada_b).reshape(DEPTH, 8, 3, D_MODEL)

    rope_tab = _rope_tables(dec_t)
    cache_k4 = cache_diff_k.reshape(dec_b, DEPTH, past, H_B * 2 * DH_B)
    cache_v4 = cache_diff_v.reshape(dec_b, DEPTH, past, H_B * DV_B)
    tiles_per_seq = dec_t // TOKEN_TILE
    ctx_row = lambda i: 0
    lat_row = lambda i: 1 + i // tiles_per_seq

    xp, xs = x_prompt, x_sample
    new_k, new_v, new_s = [], [], []
    for l in range(DEPTH):
        lw = (norm_g[l], _take_segments(w_in[l], PROJ_SEGS), _take_segments(w_in[l], MERGE_SEGS),
              hgrn_norm_g[l], diff_qn_g[l], diff_kn_g[l], diff_lambda[l], diff_subln_g[l],
              w_branch_a[l].astype(BF16), w_branch_b[l].astype(BF16),
              w_branch_c[l].astype(BF16), w_out[l].astype(BF16))
        mod3 = mod[l]
        xp, k_c, v_c, s_c = _layer(xp, mod3, ctx_row, l, lw, log_lb[l], log1m_lb[l])
        new_k.append(k_c)
        new_v.append(v_c)
        new_s.append(s_c)
        xs = _layer(xs, mod3, lat_row, l, lw, log_lb[l], log1m_lb[l],
                    ctx=(cache_k4, cache_v4, state_hgrn[:, l]), rope_tab=rope_tab)
    new_k = jnp.stack(new_k, axis=1).reshape(batch, DEPTH, seq, H_B, 2, DH_B)
    new_v = jnp.stack(new_v, axis=1).reshape(batch, DEPTH, seq, H_B, DV_B)
    new_s = jnp.stack(new_s, axis=1)
    return (xp, xs, new_k, new_v, new_s)
```

```python
import functools
import math

import numpy as np
import jax
import jax.numpy as jnp
from jax import lax
from jax.experimental import pallas as pl
from jax.experimental.pallas import tpu as pltpu

F32 = jnp.float32
BF16 = jnp.bfloat16

D_MODEL = 1024
DEPTH = 2
GRID_W = 64
EPS = 1e-6
H_A = 4
DK_A = 128
DV_A = 128
H_B = 4
DH_B = 64
DV_B = 128
G_C = 4
DG_C = 128
ROPE_BASE = 10000.0
SEG = 512
(S_QA, S_FFA, S_FBA, S_IA, S_ZA, S_QB, S_KB, S_VB, S_ZB, S_UC, S_ZC) = range(11)
S_GATE = 11
PROJ_SEGS = (S_QA, S_FFA, S_FBA, S_IA, S_QB, S_KB, S_VB, S_UC)
MERGE_SEGS = (S_ZA, S_ZB, S_ZC) + tuple(range(S_GATE, S_GATE + 6))

CHUNK = 128
SUB = 16
HEADS_PER_STEP = 2
TOKEN_TILE = 512
VMEM_LIMIT = 56 * 1024 * 1024


def _cparams(sem):
    return pltpu.CompilerParams(dimension_semantics=sem, vmem_limit_bytes=VMEM_LIMIT)


def _resident(shape):
    return pl.BlockSpec(shape, lambda *_: (0,) * len(shape), pipeline_mode=pl.Buffered(1))


def _silu(x):
    return x * jax.nn.sigmoid(x)


def _log_sigmoid(x):
    return jnp.minimum(x, 0.0) - jnp.log1p(jnp.exp(-jnp.abs(x)))


def _dot(a, b):
    return jnp.dot(a, b, preferred_element_type=F32)


def _dot_nt(a, b):
    return lax.dot_general(a, b, (((1,), (1,)), ((), ())), preferred_element_type=F32)


def _modulated_norm(x, g, mod_ref):
    y = x * lax.rsqrt(jnp.mean(x * x, axis=-1, keepdims=True) + EPS) * g
    shift = mod_ref[0, 0:1, :]
    scale = mod_ref[0, 1:2, :]
    return (y * (1.0 + scale) + shift).astype(BF16)


def _mod_kernel(c_ref, w_ref, b_ref, o_ref):
    c = c_ref[...]
    s = _silu(c).astype(BF16)
    o_ref[0] = _dot(s, w_ref[0].astype(BF16)) + b_ref[0]


def _modulation(cond8, ada_w, ada_b):
    tn = 512
    return pl.pallas_call(
        _mod_kernel,
        grid=(DEPTH, 3 * D_MODEL // tn),
        in_specs=[
            pl.BlockSpec((8, D_MODEL), lambda l, j: (0, 0)),
            pl.BlockSpec((1, D_MODEL, tn), lambda l, j: (l, 0, j)),
            pl.BlockSpec((1, 1, tn), lambda l, j: (l, 0, j)),
        ],
        out_specs=pl.BlockSpec((1, 8, tn), lambda l, j: (l, 0, j)),
        out_shape=jax.ShapeDtypeStruct((DEPTH, 8, 3 * D_MODEL), F32),
        compiler_params=_cparams(("arbitrary", "arbitrary")),
        name="adaln_mod",
    )(cond8, ada_w, ada_b.reshape(DEPTH, 1, 3 * D_MODEL))


def _proj_kernel(x_ref, mod_ref, g_ref, w_ref, la_ref, l1_ref, pa_ref, pb_ref, pc_ref):
    h = _modulated_norm(x_ref[...], g_ref[...], mod_ref)

    def seg(s):
        return _dot(h, w_ref[:, s * SEG:(s + 1) * SEG])

    def log_f(raw, d):
        a = la_ref[d:d + 1, :]
        b = l1_ref[d:d + 1, :] + _log_sigmoid(raw)
        return jnp.maximum(a, b) + jnp.log1p(jnp.exp(-jnp.abs(a - b)))

    pa_ref[:, 0 * SEG:1 * SEG] = _silu(seg(0))
    pa_ref[:, 1 * SEG:2 * SEG] = log_f(seg(1), 0)
    pa_ref[:, 2 * SEG:3 * SEG] = log_f(seg(2), 1)
    pa_ref[:, 3 * SEG:4 * SEG] = seg(3)
    for s in range(3):
        pb_ref[:, s * SEG:(s + 1) * SEG] = seg(4 + s)
    pc_ref[...] = seg(7)


def _in_proj(x2, mod3, row_of_tile, norm_g, w_proj, log_lb, log1m_lb):
    n = x2.shape[0]
    tm = TOKEN_TILE
    tok = lambda i: (i, 0)
    return pl.pallas_call(
        _proj_kernel,
        grid=(n // tm,),
        in_specs=[
            pl.BlockSpec((tm, D_MODEL), tok),
            pl.BlockSpec((1, 3, D_MODEL), lambda i: (row_of_tile(i), 0, 0)),
            _resident((1, D_MODEL)),
            _resident((D_MODEL, len(PROJ_SEGS) * SEG)),
            _resident((2, SEG)),
            _resident((2, SEG)),
        ],
        out_specs=[pl.BlockSpec((tm, 4 * SEG), tok), pl.BlockSpec((tm, 3 * SEG), tok),
                   pl.BlockSpec((tm, SEG), tok)],
        out_shape=[jax.ShapeDtypeStruct((n, 4 * SEG), F32), jax.ShapeDtypeStruct((n, 3 * SEG), F32),
                   jax.ShapeDtypeStruct((n, SEG), F32)],
        compiler_params=_cparams(("arbitrary",)),
        name="in_proj",
    )(x2, mod3, norm_g.reshape(1, D_MODEL), w_proj, log_lb, log1m_lb)


def _split3(x):
    hi = x.astype(BF16)
    r = x - hi.astype(F32)
    mid = r.astype(BF16)
    lo = (r - mid.astype(F32)).astype(BF16)
    return hi, mid, lo


def _level_exponents(g, m, reverse, diag):
    nb = CHUNK // m
    eq, ek = [], []
    ninf = jnp.full((m, 128), -jnp.inf, F32)
    for i in range(nb):
        sl = g[i * m:(i + 1) * m]
        if diag:
            r = i * m + (m // 2 - 1 if not reverse else m // 2)
            ref = g[r:r + 1]
            eq.append(sl - ref)
            ek.append(ref - sl)
        elif not reverse:
            if i % 2 == 1:
                eq.append(sl - g[i * m - 1:i * m])
                ek.append(ninf)
            else:
                eq.append(ninf)
                ek.append(g[(i + 1) * m - 1:(i + 1) * m] - sl)
        else:
            if i % 2 == 0:
                eq.append(sl - g[(i + 1) * m:(i + 1) * m + 1])
                ek.append(ninf)
            else:
                eq.append(ninf)
                ek.append(g[i * m:i * m + 1] - sl)
    return jnp.concatenate(eq, axis=0), jnp.concatenate(ek, axis=0)


def _level_masks(reverse):
    t = lax.broadcasted_iota(jnp.int32, (CHUNK, CHUNK), 0)
    s = lax.broadcasted_iota(jnp.int32, (CHUNK, CHUNK), 1)
    sh = int(math.log2(SUB))
    same = (t >> sh) == (s >> sh)
    masks = [same & ((s >= t) if reverse else (s <= t))]
    m = SUB
    while m < CHUNK:
        sh = int(math.log2(m))
        tb, sb = t >> sh, s >> sh
        if not reverse:
            masks.append(((tb & 1) == 1) & (sb == tb - 1))
        else:
            masks.append(((tb & 1) == 0) & (sb == tb + 1))
        m *= 2
    return masks


def _hgrn_chunk(lf, q, v, st, tri, masks, reverse):
    end_row = 0 if reverse else CHUNK - 1
    k = 1.0 - jnp.exp(lf)
    hi, mid, lo = _split3(lf)
    g = _dot(tri, hi) + _dot(tri, mid) + _dot(tri, lo)
    g_end = g[end_row:end_row + 1]
    v_bf = v.astype(BF16)

    qg = (q * jnp.exp(g)).astype(BF16)
    o = _dot_nt(qg, st.astype(BF16))

    a = jnp.zeros((CHUNK, CHUNK), F32)
    m = SUB
    for lvl, mask in enumerate(masks):
        eq, ek = _level_exponents(g, m, reverse, diag=(lvl == 0))
        p = _dot_nt((q * jnp.exp(eq)).astype(BF16), (k * jnp.exp(ek)).astype(BF16))
        a = jnp.where(mask, p, a)
        if lvl > 0:
            m *= 2
    o = o + _dot(a.astype(BF16), v_bf)

    kg = (k * jnp.exp(g_end - g)).astype(BF16)
    u_t = _dot(v.T.astype(BF16), kg)
    return o, jnp.exp(g_end) * st + u_t


def _hgrn_kernel(*refs, n_chunks, has_state, want_state, n_prev):
    it = iter(refs)
    q_ref, lff_ref, lfb_ref, v_ref, g_ref = (next(it) for _ in range(5))
    s0_ref = next(it) if has_state else None
    prev_refs = [next(it) for _ in range(n_prev)]
    y_ref = next(it)
    sfin_ref = next(it) if want_state else None
    o_scr, st_scr = next(it), next(it)
    hpb = HEADS_PER_STEP

    t_i = lax.broadcasted_iota(jnp.int32, (CHUNK, CHUNK), 0)
    s_i = lax.broadcasted_iota(jnp.int32, (CHUNK, CHUNK), 1)
    tris = [jnp.where(s_i <= t_i, 1.0, 0.0).astype(BF16), jnp.where(s_i >= t_i, 1.0, 0.0).astype(BF16)]
    masks = [_level_masks(False), _level_masks(True)]
    lf_refs = (lff_ref, lfb_ref)

    for hh in range(hpb):
        for d in range(2):
            if has_state:
                st_scr[2 * hh + d] = s0_ref[0, d, hh].T
            else:
                st_scr[2 * hh + d] = jnp.zeros((DV_A, DK_A), F32)

    def step(ci):
        for hh in range(hpb):
            ls = slice(hh * 128, (hh + 1) * 128)
            for d in range(2):
                c = ci if d == 0 else n_chunks - 1 - ci
                r0 = c * CHUNK
                if not isinstance(r0, int):
                    r0 = pl.multiple_of(r0, CHUNK)
                rows = pl.ds(r0, CHUNK)
                o, st = _hgrn_chunk(lf_refs[d][0, rows, ls], q_ref[0, rows, ls], v_ref[0, rows, ls],
                                    st_scr[2 * hh + d], tris[d], masks[d], reverse=(d == 1))
                o_scr[d, rows, ls] = o
                st_scr[2 * hh + d] = st

    if n_chunks <= 2:
        for ci in range(n_chunks):
            step(ci)
    else:
        def body(i, carry):
            step(2 * i)
            step(2 * i + 1)
            return carry
        lax.fori_loop(0, n_chunks // 2, body, 0)

    for j, p_ref in enumerate(prev_refs):
        sfin_ref[0, j] = p_ref[0]
    for hh in range(hpb):
        ls = slice(hh * 128, (hh + 1) * 128)
        if want_state:
            for d in range(2):
                if n_prev:
                    sfin_ref[0, n_prev, d, hh] = st_scr[2 * hh + d].T
                else:
                    sfin_ref[0, d, hh] = st_scr[2 * hh + d].T
        o = o_scr[0, :, ls] + o_scr[1, :, ls]
        y_ref[0, :, ls] = o * lax.rsqrt(jnp.mean(o * o, axis=-1, keepdims=True) + EPS) * g_ref[0, :, ls]


def _hgrn(pa3, hgrn_g, s0, want_state, prev_states=()):
    b, t, _ = pa3.shape
    n_chunks = t // CHUNK
    has_state = s0 is not None
    hpb = HEADS_PER_STEP
    w = 128 * hpb
    per_seg = SEG // w
    n_prev = len(prev_states)

    def col(seg):
        return pl.BlockSpec((1, t, w), lambda bi, hp: (bi, 0, seg * per_seg + hp))

    in_specs = [col(0), col(1), col(2), col(3),
                pl.BlockSpec((1, 1, w), lambda bi, hp: (hp, 0, 0))]
    args = [pa3, pa3, pa3, pa3, hgrn_g.reshape(H_A // hpb, 1, w)]
    st_spec = pl.BlockSpec((1, 2, hpb, DK_A, DV_A), lambda bi, hp: (bi, 0, hp, 0, 0))
    if has_state:
        in_specs.append(st_spec)
        args.append(s0)
    in_specs += [st_spec] * n_prev
    args += list(prev_states)
    out_specs = [pl.BlockSpec((1, t, w), lambda bi, hp: (bi, 0, hp))]
    out_shape = [jax.ShapeDtypeStruct((b, t, H_A * DV_A), F32)]
    if want_state and n_prev:
        out_specs.append(pl.BlockSpec((1, n_prev + 1, 2, hpb, DK_A, DV_A),
                                      lambda bi, hp: (bi, 0, 0, hp, 0, 0)))
        out_shape.append(jax.ShapeDtypeStruct((b, n_prev + 1, 2, H_A, DK_A, DV_A), F32))
    elif want_state:
        out_specs.append(st_spec)
        out_shape.append(jax.ShapeDtypeStruct((b, 2, H_A, DK_A, DV_A), F32))
    res = pl.pallas_call(
        functools.partial(_hgrn_kernel, n_chunks=n_chunks, has_state=has_state,
                          want_state=want_state, n_prev=n_prev),
        grid=(b, H_A // hpb),
        in_specs=in_specs,
        out_specs=out_specs,
        out_shape=out_shape,
        scratch_shapes=[pltpu.VMEM((2, t, w), F32), pltpu.VMEM((2 * hpb, DV_A, DK_A), F32)],
        compiler_params=_cparams(("arbitrary", "arbitrary")),
        name="hgrn2",
    )(*args)
    return res if want_state else (res[0], None)


def _rms64(x, g):
    lane = lax.broadcasted_iota(jnp.int32, x.shape, 1)
    lo = lane < DH_B
    x2 = x * x
    s0 = jnp.sum(jnp.where(lo, x2, 0.0), axis=-1, keepdims=True)
    s1 = jnp.sum(jnp.where(lo, 0.0, x2), axis=-1, keepdims=True)
    inv = jnp.where(lo, lax.rsqrt(s0 / DH_B + EPS), lax.rsqrt(s1 / DH_B + EPS))
    return x * inv * g


def _rope(x, cos, sin):
    lane = lax.broadcasted_iota(jnp.int32, x.shape, 1)
    first = (lane & 31) < 16
    partner = jnp.where(first, pltpu.roll(x, 112, 1), pltpu.roll(x, 16, 1))
    return x * cos + partner * sin


def _attn_kernel(*refs, t_self, t_ctx, lam_init, latent, kv_out, n_prev):
    it = iter(refs)
    q_ref, k_ref, v_ref = (next(it) for _ in range(3))
    qg_ref, kg_ref, sg_ref, lam_ref = (next(it) for _ in range(4))
    if latent:
        cq_ref, sq_ref, ck_ref, sk_ref, kc_ref, vc_ref = (next(it) for _ in range(6))
    prev_k = [next(it) for _ in range(n_prev)]
    prev_v = [next(it) for _ in range(n_prev)]
    y_ref = next(it)
    kout_ref = next(it) if kv_out else None
    vout_ref = next(it) if kv_out == "stack" else None
    k_scr, v_scr = next(it), next(it)

    qi = pl.program_id(1)

    @pl.when(qi == 0)
    def _():
        for j in range(n_prev):
            kout_ref[0, j] = prev_k[j][0]
            vout_ref[0, j] = prev_v[j][0]
        if kv_out == "stack":
            vout_ref[0, n_prev] = v_ref[0]
        for h in range(H_B):
            ls = slice(h * 128, (h + 1) * 128)
            kn = _rms64(k_ref[0, :, ls], kg_ref[...])
            if kv_out == "stack":
                kout_ref[0, n_prev, :, ls] = kn
            elif kv_out:
                kout_ref[0, :, ls] = kn
            if latent:
                kn = _rope(kn, ck_ref[...], sk_ref[...])
                k_scr[t_self:t_self + t_ctx, ls] = kc_ref[0, 0, :, ls].astype(BF16)
                v_scr[t_self:t_self + t_ctx, ls] = vc_ref[0, 0, :, ls].astype(BF16)
            k_scr[0:t_self, ls] = kn.astype(BF16)
            v_scr[0:t_self, ls] = v_ref[0, :, ls].astype(BF16)

    lp = lam_ref[...]
    l01 = jnp.sum(lp[0:1] * lp[1:2], axis=-1, keepdims=True)
    l23 = jnp.sum(lp[2:3] * lp[3:4], axis=-1, keepdims=True)
    lam = jnp.exp(l01) - jnp.exp(l23) + lam_init

    for h in range(H_B):
        ls = slice(h * 128, (h + 1) * 128)
        qn = _rms64(q_ref[0, :, ls], qg_ref[...])
        if latent:
            qn = _rope(qn, cq_ref[...], sq_ref[...])
        qn = qn * (DH_B ** -0.5)
        lane = lax.broadcasted_iota(jnp.int32, qn.shape, 1)
        kf = k_scr[:, ls]

        def exps(qm):
            s = _dot_nt(qm.astype(BF16), kf)
            e = jnp.exp(s - jnp.max(s, axis=-1, keepdims=True))
            return e, jnp.sum(e, axis=-1, keepdims=True)

        e0, z0 = exps(jnp.where(lane < DH_B, qn, 0.0))
        e1, z1 = exps(jnp.where(lane < DH_B, 0.0, qn))
        w = (e0 * (1.0 / z0) - e1 * (lam / z1)).astype(BF16)
        o = _dot(w, v_scr[:, ls])
        y = o * lax.rsqrt(jnp.mean(o * o, axis=-1, keepdims=True) + EPS) * sg_ref[...]
        y_ref[0, :, ls] = y * (1.0 - lam_init)


def _diff_attn(pb3, layer, qn_g, kn_g, sub_g, lam_p, rope_tab, cache_k, cache_v, kv_out,
               prev_k=(), prev_pb=()):
    b, t, _ = pb3.shape
    latent = rope_tab is not None
    tq = 256
    t_ctx = cache_k.shape[2] if latent else 0
    lam_init = 0.8 - 0.6 * math.exp(-0.3 * layer)
    n_prev = len(prev_k)
    w = H_B * 128

    def small(shape):
        return pl.BlockSpec(shape, lambda bi, qi: (0,) * len(shape))

    seq_blk = lambda c: pl.BlockSpec((1, t, w), lambda bi, qi: (bi, 0, c))
    in_specs = [
        pl.BlockSpec((1, tq, w), lambda bi, qi: (bi, qi, 0)),
        seq_blk(1), seq_blk(2),
        small((1, 128)), small((1, 128)), small((1, 128)), small((4, DH_B)),
    ]
    args = [pb3, pb3, pb3, jnp.tile(qn_g, 2).reshape(1, 128), jnp.tile(kn_g, 2).reshape(1, 128),
            sub_g.reshape(1, 128), lam_p]
    if latent:
        cos_t, sin_t = rope_tab
        in_specs += [
            pl.BlockSpec((tq, 128), lambda bi, qi: (qi, 0)),
            pl.BlockSpec((tq, 128), lambda bi, qi: (qi, 0)),
            small((t, 128)), small((t, 128)),
            pl.BlockSpec((1, 1, t_ctx, w), lambda bi, qi: (bi, layer, 0, 0)),
            pl.BlockSpec((1, 1, t_ctx, w), lambda bi, qi: (bi, layer, 0, 0)),
        ]
        args += [cos_t, sin_t, cos_t, sin_t, cache_k, cache_v]
    in_specs += [seq_blk(0)] * n_prev + [seq_blk(2)] * n_prev
    args += list(prev_k) + list(prev_pb)
    out_specs = [pl.BlockSpec((1, tq, w), lambda bi, qi: (bi, qi, 0))]
    out_shape = [jax.ShapeDtypeStruct((b, t, w), F32)]
    if kv_out == "stack":
        stack_spec = pl.BlockSpec((1, n_prev + 1, t, w), lambda bi, qi: (bi, 0, 0, 0))
        out_specs += [stack_spec, stack_spec]
        out_shape += [jax.ShapeDtypeStruct((b, n_prev + 1, t, w), F32)] * 2
    elif kv_out:
        out_specs.append(seq_blk(0))
        out_shape.append(jax.ShapeDtypeStruct((b, t, w), F32))
    return pl.pallas_call(
        functools.partial(_attn_kernel, t_self=t, t_ctx=t_ctx, lam_init=lam_init,
                          latent=latent, kv_out=kv_out, n_prev=n_prev),
        grid=(b, t // tq),
        in_specs=in_specs,
        out_specs=out_specs,
        out_shape=out_shape,
        scratch_shapes=[pltpu.VMEM((t + t_ctx, w), BF16), pltpu.VMEM((t + t_ctx, w), BF16)],
        compiler_params=_cparams(("arbitrary", "arbitrary")),
        name="diff_attn",
    )(*args)


def _dft_tables(t):
    def cs(n):
        idx = np.arange(n, dtype=np.int64)
        ang = 2.0 * np.pi * ((idx[:, None] * idx[None, :]) % n).astype(np.float64) / n
        return np.cos(ang), np.sin(ang)

    cc, sc = cs(DG_C)
    ct, st = cs(t)
    chan = jnp.asarray(np.concatenate([cc, sc], axis=1), dtype=F32)
    pos = jnp.asarray(np.concatenate([ct, -st], axis=1), dtype=F32)
    return chan.astype(BF16), pos.astype(BF16)


def _fourier_kernel(u_ref, chan_ref, pos_ref, y_ref, uc_scr, *, t, scale):
    ri = pl.program_id(1)

    @pl.when(ri == 0)
    def _():
        for g in range(G_C):
            ls = slice(g * DG_C, (g + 1) * DG_C)
            u1 = _dot(u_ref[0, :, ls].astype(BF16), chan_ref[...])
            uc_scr[0:t, ls] = u1[:, 0:DG_C].astype(BF16)
            uc_scr[t:2 * t, ls] = u1[:, DG_C:2 * DG_C].astype(BF16)

    y_ref[0] = _dot(pos_ref[...], uc_scr[...]) * scale


def _fourier(pc3):
    b, t, w = pc3.shape
    tr = min(t, 512)
    chan, pos = _dft_tables(t)
    scale = 1.0 / math.sqrt(t * DG_C)
    return pl.pallas_call(
        functools.partial(_fourier_kernel, t=t, scale=scale),
        grid=(b, t // tr),
        in_specs=[
            pl.BlockSpec((1, t, w), lambda bi, ri: (bi, 0, 0)),
            _resident((DG_C, 2 * DG_C)),
            pl.BlockSpec((tr, 2 * t), lambda bi, ri: (ri, 0)),
        ],
        out_specs=pl.BlockSpec((1, tr, w), lambda bi, ri: (bi, ri, 0)),
        out_shape=jax.ShapeDtypeStruct((b, t, w), F32),
        scratch_shapes=[pltpu.VMEM((2 * t, w), BF16)],
        compiler_params=_cparams(("arbitrary", "arbitrary")),
        name="fourier_mix",
    )(pc3, chan, pos)


def _merge_kernel(x_ref, mod_ref, g_ref, oa_ref, ob_ref, oc_ref, wm_ref, wa_ref, wb_ref, wc_ref,
                  wo_ref, o_ref):
    x = x_ref[...]
    h = _modulated_norm(x, g_ref[...], mod_ref)

    def seg(s):
        return _dot(h, wm_ref[:, s * SEG:(s + 1) * SEG])

    ya = (oa_ref[...] * _silu(seg(0))).astype(BF16)
    yb = (ob_ref[...] * _silu(seg(1))).astype(BF16)
    yc = (oc_ref[...] * _silu(seg(2))).astype(BF16)
    parts = []
    for n in range(2):
        cols = slice(n * SEG, (n + 1) * SEG)
        m = jax.nn.sigmoid(seg(3 + n)) * _dot(ya, wa_ref[:, cols])
        m = m + jax.nn.sigmoid(seg(5 + n)) * _dot(yb, wb_ref[:, cols])
        m = m + jax.nn.sigmoid(seg(7 + n)) * _dot(yc, wc_ref[:, cols])
        parts.append(m.astype(BF16))
    merged = jnp.concatenate(parts, axis=1)
    gate = mod_ref[0, 2:3, :]
    o_ref[...] = x + gate * _dot(merged, wo_ref[...])


def _merge(x2, mod3, row_of_tile, norm_g, oa, ob, oc, w_merge, wa, wb, wc, wo):
    n = x2.shape[0]
    tm = TOKEN_TILE
    tok = lambda i: (i, 0)
    in_specs = [
        pl.BlockSpec((tm, D_MODEL), tok),
        pl.BlockSpec((1, 3, D_MODEL), lambda i: (row_of_tile(i), 0, 0)),
        _resident((1, D_MODEL)),
        pl.BlockSpec((tm, SEG), tok), pl.BlockSpec((tm, SEG), tok), pl.BlockSpec((tm, SEG), tok),
        _resident((D_MODEL, len(MERGE_SEGS) * SEG)),
        _resident((SEG, D_MODEL)), _resident((SEG, D_MODEL)), _resident((SEG, D_MODEL)),
        _resident((D_MODEL, D_MODEL)),
    ]
    return pl.pallas_call(
        _merge_kernel,
        grid=(n // tm,),
        in_specs=in_specs,
        out_specs=pl.BlockSpec((tm, D_MODEL), tok),
        out_shape=jax.ShapeDtypeStruct((n, D_MODEL), F32),
        compiler_params=_cparams(("arbitrary",)),
        name="merge_out",
    )(x2, mod3, norm_g.reshape(1, D_MODEL), oa, ob, oc, w_merge, wa, wb, wc, wo)


def _rope_tables(t):
    nf = DH_B // 4
    n_rows = t // GRID_W
    pos_row = jnp.repeat(jnp.arange(n_rows, dtype=F32), GRID_W)
    pos_col = jnp.tile(jnp.arange(GRID_W, dtype=F32), n_rows)
    inv = ROPE_BASE ** (-jnp.arange(nf, dtype=F32) / nf)
    ar = pos_row[:, None] * inv[None, :]
    ac = pos_col[:, None] * inv[None, :]
    cos64 = jnp.concatenate([jnp.cos(ar), jnp.cos(ar), jnp.cos(ac), jnp.cos(ac)], axis=1)
    sin64 = jnp.concatenate([-jnp.sin(ar), jnp.sin(ar), -jnp.sin(ac), jnp.sin(ac)], axis=1)
    return jnp.tile(cos64, (1, 2)), jnp.tile(sin64, (1, 2))


def _take_segments(w, segs):
    return jnp.concatenate([w[:, s * SEG:(s + 1) * SEG] for s in segs], axis=1).astype(BF16)


def _layer(x, mod3, row_of_tile, layer, lw, log_lb, log1m_lb, ctx=None, rope_tab=None, prev=None):
    (norm_g, w_proj, w_merge, hgrn_g, qn_g, kn_g, lam_p, sub_g, wa, wb, wc, wo) = lw
    b, t, _ = x.shape
    x2 = x.reshape(b * t, D_MODEL)
    pa, pb, pc = _in_proj(x2, mod3, row_of_tile, norm_g, w_proj, log_lb, log1m_lb)
    pa3 = pa.reshape(b, t, 4 * SEG)
    pb3 = pb.reshape(b, t, 3 * SEG)
    is_ctx = ctx is None
    if not is_ctx:
        oa, _ = _hgrn(pa3, hgrn_g, ctx[2], want_state=False)
        (ob,) = _diff_attn(pb3, layer, qn_g, kn_g, sub_g, lam_p, rope_tab, ctx[0], ctx[1], None)
    elif prev is None:
        oa, s_out = _hgrn(pa3, hgrn_g, None, want_state=True)
        ob, k_out = _diff_attn(pb3, layer, qn_g, kn_g, sub_g, lam_p, None, None, None, "plain")
        v_out = pb3
    else:
        oa, s_out = _hgrn(pa3, hgrn_g, None, want_state=True, prev_states=prev[2])
        ob, k_out, v_out = _diff_attn(pb3, layer, qn_g, kn_g, sub_g, lam_p, None, None, None,
                                      "stack", prev_k=prev[0], prev_pb=prev[1])
    oc = _fourier(pc.reshape(b, t, SEG))
    x_new = _merge(x2, mod3, row_of_tile, norm_g, oa.reshape(b * t, SEG), ob.reshape(b * t, SEG),
                   oc.reshape(b * t, SEG), w_merge, wa, wb, wc, wo)
    x_new = x_new.reshape(b, t, D_MODEL)
    if is_ctx:
        return x_new, k_out, v_out, s_out
    return x_new


def kernel(x_prompt, x_sample, c, cache_diff_k, cache_diff_v, state_hgrn, c_ctx, ada_w, ada_b,
           norm_g, w_in, hgrn_lb, hgrn_norm_g, diff_qn_g, diff_kn_g, diff_lambda, diff_subln_g,
           w_branch_a, w_branch_b, w_branch_c, w_out):
    batch, seq, _ = x_prompt.shape
    dec_b, dec_t, _ = x_sample.shape
    past = cache_diff_k.shape[2]

    lb_all = jnp.cumsum(jax.nn.softmax(hgrn_lb.astype(F32), axis=0), axis=0)
    lb_all = lb_all - lb_all[0]
    log_lb = jnp.log(lb_all)
    log1m_lb = jnp.log1p(-lb_all)

    cond8 = jnp.zeros((8, D_MODEL), F32).at[0].set(c_ctx).at[1:1 + dec_b].set(c)
    mod = _modulation(cond8, ada_w, ada_b).reshape(DEPTH, 8, 3, D_MODEL)

    rope_tab = _rope_tables(dec_t)
    cache_k4 = cache_diff_k.reshape(dec_b, DEPTH, past, H_B * 2 * DH_B)
    cache_v4 = cache_diff_v.reshape(dec_b, DEPTH, past, H_B * DV_B)
    tiles_per_seq = dec_t // TOKEN_TILE
    ctx_row = lambda i: 0
    lat_row = lambda i: 1 + i // tiles_per_seq

    xp, xs = x_prompt, x_sample
    prev_k, prev_pb, prev_s = [], [], []
    for l in range(DEPTH):
        lw = (norm_g[l], _take_segments(w_in[l], PROJ_SEGS), _take_segments(w_in[l], MERGE_SEGS),
              hgrn_norm_g[l], diff_qn_g[l], diff_kn_g[l], diff_lambda[l], diff_subln_g[l],
              w_branch_a[l].astype(BF16), w_branch_b[l].astype(BF16),
              w_branch_c[l].astype(BF16), w_out[l].astype(BF16))
        mod3 = mod[l]
        last = l == DEPTH - 1
        xp, k_c, v_c, s_c = _layer(xp, mod3, ctx_row, l, lw, log_lb[l], log1m_lb[l],
                                   prev=(prev_k, prev_pb, prev_s) if last and l > 0 else None)
        if not last:
            prev_k.append(k_c)
            prev_pb.append(v_c)
            prev_s.append(s_c)
        xs = _layer(xs, mod3, lat_row, l, lw, log_lb[l], log1m_lb[l],
                    ctx=(cache_k4, cache_v4, state_hgrn[:, l]), rope_tab=rope_tab)
    new_k = k_c.reshape(batch, DEPTH, seq, H_B, 2, DH_B)
    new_v = v_c.reshape(batch, DEPTH, seq, H_B, DV_B)
    return (xp, xs, new_k, new_v, s_c)
```

```python
import functools
import math

import numpy as np
import jax
import jax.numpy as jnp
from jax import lax
from jax.experimental import pallas as pl
from jax.experimental.pallas import tpu as pltpu

F32 = jnp.float32
BF16 = jnp.bfloat16

D_MODEL = 1024
DEPTH = 2
GRID_W = 64
EPS = 1e-6
H_A = 4
DK_A = 128
DV_A = 128
H_B = 4
DH_B = 64
DV_B = 128
G_C = 4
DG_C = 128
ROPE_BASE = 10000.0
SEG = 512
(S_QA, S_FFA, S_FBA, S_IA, S_ZA, S_QB, S_KB, S_VB, S_ZB, S_UC, S_ZC) = range(11)
S_GATE = 11
PROJ_SEGS = (S_QA, S_FFA, S_FBA, S_IA, S_QB, S_KB, S_VB, S_UC)
MERGE_SEGS = (S_ZA, S_ZB, S_ZC) + tuple(range(S_GATE, S_GATE + 6))

CHUNK = 128
SUB = 16
HEADS_PER_STEP = 2
TOKEN_TILE = 512
VMEM_LIMIT = 56 * 1024 * 1024


def _cparams(sem):
    return pltpu.CompilerParams(dimension_semantics=sem, vmem_limit_bytes=VMEM_LIMIT)


def _resident(shape):
    return pl.BlockSpec(shape, lambda *_: (0,) * len(shape), pipeline_mode=pl.Buffered(1))


def _silu(x):
    return x * jax.nn.sigmoid(x)


def _log_sigmoid(x):
    return jnp.minimum(x, 0.0) - jnp.log1p(jnp.exp(-jnp.abs(x)))


def _dot(a, b):
    return jnp.dot(a, b, preferred_element_type=F32)


def _dot_nt(a, b):
    return lax.dot_general(a, b, (((1,), (1,)), ((), ())), preferred_element_type=F32)


def _modulated_norm(x, g, mod_ref):
    y = x * lax.rsqrt(jnp.mean(x * x, axis=-1, keepdims=True) + EPS) * g
    shift = mod_ref[0, 0:1, :]
    scale = mod_ref[0, 1:2, :]
    return (y * (1.0 + scale) + shift).astype(BF16)


def _mod_kernel(c_ref, w_ref, b_ref, o_ref):
    c = c_ref[...]
    s = _silu(c).astype(BF16)
    o_ref[0] = _dot(s, w_ref[0].astype(BF16)) + b_ref[0]


def _modulation(cond8, ada_w, ada_b):
    tn = 512
    return pl.pallas_call(
        _mod_kernel,
        grid=(DEPTH, 3 * D_MODEL // tn),
        in_specs=[
            pl.BlockSpec((8, D_MODEL), lambda l, j: (0, 0)),
            pl.BlockSpec((1, D_MODEL, tn), lambda l, j: (l, 0, j)),
            pl.BlockSpec((1, 1, tn), lambda l, j: (l, 0, j)),
        ],
        out_specs=pl.BlockSpec((1, 8, tn), lambda l, j: (l, 0, j)),
        out_shape=jax.ShapeDtypeStruct((DEPTH, 8, 3 * D_MODEL), F32),
        compiler_params=_cparams(("arbitrary", "arbitrary")),
        name="adaln_mod",
    )(cond8, ada_w, ada_b.reshape(DEPTH, 1, 3 * D_MODEL))


def _proj_kernel(x_ref, mod_ref, g_ref, w_ref, la_ref, l1_ref, pa_ref, pb_ref, pc_ref):
    h = _modulated_norm(x_ref[...], g_ref[...], mod_ref)

    def seg(s):
        return _dot(h, w_ref[:, s * SEG:(s + 1) * SEG])

    def log_f(raw, d):
        a = la_ref[d:d + 1, :]
        b = l1_ref[d:d + 1, :] + _log_sigmoid(raw)
        return jnp.maximum(a, b) + jnp.log1p(jnp.exp(-jnp.abs(a - b)))

    pa_ref[:, 0 * SEG:1 * SEG] = _silu(seg(0))
    pa_ref[:, 1 * SEG:2 * SEG] = log_f(seg(1), 0)
    pa_ref[:, 2 * SEG:3 * SEG] = log_f(seg(2), 1)
    pa_ref[:, 3 * SEG:4 * SEG] = seg(3)
    for s in range(3):
        pb_ref[:, s * SEG:(s + 1) * SEG] = seg(4 + s)
    pc_ref[...] = seg(7)


def _in_proj(x2, mod3, row_of_tile, norm_g, w_proj, log_lb, log1m_lb):
    n = x2.shape[0]
    tm = TOKEN_TILE
    tok = lambda i: (i, 0)
    return pl.pallas_call(
        _proj_kernel,
        grid=(n // tm,),
        in_specs=[
            pl.BlockSpec((tm, D_MODEL), tok),
            pl.BlockSpec((1, 3, D_MODEL), lambda i: (row_of_tile(i), 0, 0)),
            _resident((1, D_MODEL)),
            _resident((D_MODEL, len(PROJ_SEGS) * SEG)),
            _resident((2, SEG)),
            _resident((2, SEG)),
        ],
        out_specs=[pl.BlockSpec((tm, 4 * SEG), tok), pl.BlockSpec((tm, 3 * SEG), tok),
                   pl.BlockSpec((tm, SEG), tok)],
        out_shape=[jax.ShapeDtypeStruct((n, 4 * SEG), F32), jax.ShapeDtypeStruct((n, 3 * SEG), F32),
                   jax.ShapeDtypeStruct((n, SEG), F32)],
        compiler_params=_cparams(("arbitrary",)),
        name="in_proj",
    )(x2, mod3, norm_g.reshape(1, D_MODEL), w_proj, log_lb, log1m_lb)


def _split3(x):
    hi = x.astype(BF16)
    r = x - hi.astype(F32)
    mid = r.astype(BF16)
    lo = (r - mid.astype(F32)).astype(BF16)
    return hi, mid, lo


def _level_exponents(g, m, reverse, diag):
    nb = CHUNK // m
    eq, ek = [], []
    ninf = jnp.full((m, 128), -jnp.inf, F32)
    for i in range(nb):
        sl = g[i * m:(i + 1) * m]
        if diag:
            r = i * m + (m // 2 - 1 if not reverse else m // 2)
            ref = g[r:r + 1]
            eq.append(sl - ref)
            ek.append(ref - sl)
        elif not reverse:
            if i % 2 == 1:
                eq.append(sl - g[i * m - 1:i * m])
                ek.append(ninf)
            else:
                eq.append(ninf)
                ek.append(g[(i + 1) * m - 1:(i + 1) * m] - sl)
        else:
            if i % 2 == 0:
                eq.append(sl - g[(i + 1) * m:(i + 1) * m + 1])
                ek.append(ninf)
            else:
                eq.append(ninf)
                ek.append(g[i * m:i * m + 1] - sl)
    return jnp.concatenate(eq, axis=0), jnp.concatenate(ek, axis=0)


def _level_masks(reverse):
    t = lax.broadcasted_iota(jnp.int32, (CHUNK, CHUNK), 0)
    s = lax.broadcasted_iota(jnp.int32, (CHUNK, CHUNK), 1)
    sh = int(math.log2(SUB))
    same = (t >> sh) == (s >> sh)
    masks = [same & ((s >= t) if reverse else (s <= t))]
    m = SUB
    while m < CHUNK:
        sh = int(math.log2(m))
        tb, sb = t >> sh, s >> sh
        if not reverse:
            masks.append(((tb & 1) == 1) & (sb == tb - 1))
        else:
            masks.append(((tb & 1) == 0) & (sb == tb + 1))
        m *= 2
    return masks


def _hgrn_chunk(lf, q, v, st, tri, masks, reverse):
    end_row = 0 if reverse else CHUNK - 1
    k = 1.0 - jnp.exp(lf)
    g = _dot(tri, jnp.concatenate(_split3(lf), axis=0))
    g_end = g[end_row:end_row + 1]
    vt_bf = v.T.astype(BF16)

    a = jnp.zeros((CHUNK, CHUNK), F32)
    m = SUB
    for lvl, mask in enumerate(masks):
        eq, ek = _level_exponents(g, m, reverse, diag=(lvl == 0))
        p = _dot_nt((q * jnp.exp(eq)).astype(BF16), (k * jnp.exp(ek)).astype(BF16))
        a = jnp.where(mask, p, a)
        if lvl > 0:
            m *= 2

    qg = (q * jnp.exp(g)).astype(BF16)
    o = _dot_nt(jnp.concatenate([qg, a.astype(BF16)], axis=1),
                jnp.concatenate([st.astype(BF16), vt_bf], axis=1))

    kg = (k * jnp.exp(g_end - g)).astype(BF16)
    u_t = _dot(vt_bf, kg)
    return o, jnp.exp(g_end) * st + u_t


def _hgrn_kernel(*refs, n_chunks, has_state, want_state, n_prev):
    it = iter(refs)
    q_ref, lff_ref, lfb_ref, v_ref, g_ref = (next(it) for _ in range(5))
    s0_ref = next(it) if has_state else None
    prev_refs = [next(it) for _ in range(n_prev)]
    y_ref = next(it)
    sfin_ref = next(it) if want_state else None
    o_scr, st_scr = next(it), next(it)
    hpb = HEADS_PER_STEP

    t_i = lax.broadcasted_iota(jnp.int32, (CHUNK, CHUNK), 0)
    s_i = lax.broadcasted_iota(jnp.int32, (CHUNK, CHUNK), 1)
    tris = [jnp.where(s_i <= t_i, 1.0, 0.0).astype(BF16), jnp.where(s_i >= t_i, 1.0, 0.0).astype(BF16)]
    tris = [jnp.concatenate([tr, tr, tr], axis=1) for tr in tris]
    masks = [_level_masks(False), _level_masks(True)]
    lf_refs = (lff_ref, lfb_ref)

    for hh in range(hpb):
        for d in range(2):
            if has_state:
                st_scr[2 * hh + d] = s0_ref[0, d, hh].T
            else:
                st_scr[2 * hh + d] = jnp.zeros((DV_A, DK_A), F32)

    def step(ci):
        for hh in range(hpb):
            ls = slice(hh * 128, (hh + 1) * 128)
            for d in range(2):
                c = ci if d == 0 else n_chunks - 1 - ci
                r0 = c * CHUNK
                if not isinstance(r0, int):
                    r0 = pl.multiple_of(r0, CHUNK)
                rows = pl.ds(r0, CHUNK)
                o, st = _hgrn_chunk(lf_refs[d][0, rows, ls], q_ref[0, rows, ls], v_ref[0, rows, ls],
                                    st_scr[2 * hh + d], tris[d], masks[d], reverse=(d == 1))
                o_scr[d, rows, ls] = o
                st_scr[2 * hh + d] = st

    if n_chunks <= 2:
        for ci in range(n_chunks):
            step(ci)
    else:
        def body(i, carry):
            step(2 * i)
            step(2 * i + 1)
            return carry
        lax.fori_loop(0, n_chunks // 2, body, 0)

    for j, p_ref in enumerate(prev_refs):
        sfin_ref[0, j] = p_ref[0]
    for hh in range(hpb):
        ls = slice(hh * 128, (hh + 1) * 128)
        if want_state:
            for d in range(2):
                if n_prev:
                    sfin_ref[0, n_prev, d, hh] = st_scr[2 * hh + d].T
                else:
                    sfin_ref[0, d, hh] = st_scr[2 * hh + d].T
        o = o_scr[0, :, ls] + o_scr[1, :, ls]
        y_ref[0, :, ls] = o * lax.rsqrt(jnp.mean(o * o, axis=-1, keepdims=True) + EPS) * g_ref[0, :, ls]


def _hgrn(pa3, hgrn_g, s0, want_state, prev_states=()):
    b, t, _ = pa3.shape
    n_chunks = t // CHUNK
    has_state = s0 is not None
    hpb = HEADS_PER_STEP
    w = 128 * hpb
    per_seg = SEG // w
    n_prev = len(prev_states)

    def col(seg):
        return pl.BlockSpec((1, t, w), lambda bi, hp: (bi, 0, seg * per_seg + hp))

    in_specs = [col(0), col(1), col(2), col(3),
                pl.BlockSpec((1, 1, w), lambda bi, hp: (hp, 0, 0))]
    args = [pa3, pa3, pa3, pa3, hgrn_g.reshape(H_A // hpb, 1, w)]
    st_spec = pl.BlockSpec((1, 2, hpb, DK_A, DV_A), lambda bi, hp: (bi, 0, hp, 0, 0))
    if has_state:
        in_specs.append(st_spec)
        args.append(s0)
    in_specs += [st_spec] * n_prev
    args += list(prev_states)
    out_specs = [pl.BlockSpec((1, t, w), lambda bi, hp: (bi, 0, hp))]
    out_shape = [jax.ShapeDtypeStruct((b, t, H_A * DV_A), F32)]
    if want_state and n_prev:
        out_specs.append(pl.BlockSpec((1, n_prev + 1, 2, hpb, DK_A, DV_A),
                                      lambda bi, hp: (bi, 0, 0, hp, 0, 0)))
        out_shape.append(jax.ShapeDtypeStruct((b, n_prev + 1, 2, H_A, DK_A, DV_A), F32))
    elif want_state:
        out_specs.append(st_spec)
        out_shape.append(jax.ShapeDtypeStruct((b, 2, H_A, DK_A, DV_A), F32))
    res = pl.pallas_call(
        functools.partial(_hgrn_kernel, n_chunks=n_chunks, has_state=has_state,
                          want_state=want_state, n_prev=n_prev),
        grid=(b, H_A // hpb),
        in_specs=in_specs,
        out_specs=out_specs,
        out_shape=out_shape,
        scratch_shapes=[pltpu.VMEM((2, t, w), F32), pltpu.VMEM((2 * hpb, DV_A, DK_A), F32)],
        compiler_params=_cparams(("arbitrary", "arbitrary")),
        name="hgrn2",
    )(*args)
    return res if want_state else (res[0], None)


def _half_sum_matrix():
    r = lax.broadcasted_iota(jnp.int32, (128, 128), 0)
    c = lax.broadcasted_iota(jnp.int32, (128, 128), 1)
    return jnp.where((r >> 6) == (c >> 6), 1.0, 0.0).astype(BF16)


def _rms64(x, g, half_sum):
    x2 = x * x
    hi = x2.astype(BF16)
    lo = (x2 - hi.astype(F32)).astype(BF16)
    ss = _dot(hi, half_sum) + _dot(lo, half_sum)
    return x * lax.rsqrt(ss / DH_B + EPS) * g


def _rope(x, cos, sin):
    lane = lax.broadcasted_iota(jnp.int32, x.shape, 1)
    first = (lane & 31) < 16
    partner = jnp.where(first, pltpu.roll(x, 112, 1), pltpu.roll(x, 16, 1))
    return x * cos + partner * sin


def _attn_kernel(*refs, t_self, t_ctx, lam_init, latent, kv_out, n_prev):
    it = iter(refs)
    q_ref, k_ref, v_ref = (next(it) for _ in range(3))
    qg_ref, kg_ref, sg_ref, lam_ref = (next(it) for _ in range(4))
    if latent:
        cq_ref, sq_ref, ck_ref, sk_ref, kc_ref, vc_ref = (next(it) for _ in range(6))
    prev_k = [next(it) for _ in range(n_prev)]
    prev_v = [next(it) for _ in range(n_prev)]
    y_ref = next(it)
    kout_ref = next(it) if kv_out else None
    vout_ref = next(it) if kv_out == "stack" else None
    k_scr, v_scr = next(it), next(it)

    qi = pl.program_id(1)
    half_sum = _half_sum_matrix()
    t_all = t_self + t_ctx

    @pl.when(qi == 0)
    def _():
        for j in range(n_prev):
            kout_ref[0, j] = prev_k[j][0]
            vout_ref[0, j] = prev_v[j][0]
        if kv_out == "stack":
            vout_ref[0, n_prev] = v_ref[0]
        for h in range(H_B):
            ls = slice(h * 128, (h + 1) * 128)
            vs = slice(h * 256, h * 256 + 128)
            kn = _rms64(k_ref[0, :, ls], kg_ref[...], half_sum)
            if kv_out == "stack":
                kout_ref[0, n_prev, :, ls] = kn
            elif kv_out:
                kout_ref[0, :, ls] = kn
            if latent:
                kn = _rope(kn, ck_ref[...], sk_ref[...])
                k_scr[t_self:t_all, ls] = kc_ref[0, 0, :, ls].astype(BF16)
                v_scr[t_self:t_all, vs] = vc_ref[0, 0, :, ls].astype(BF16)
            k_scr[0:t_self, ls] = kn.astype(BF16)
            v_scr[0:t_self, vs] = v_ref[0, :, ls].astype(BF16)
            v_scr[:, h * 256 + 128:(h + 1) * 256] = jnp.ones((t_all, 128), BF16)

    lp = lam_ref[...]
    l01 = jnp.sum(lp[0:1] * lp[1:2], axis=-1, keepdims=True)
    l23 = jnp.sum(lp[2:3] * lp[3:4], axis=-1, keepdims=True)
    lam = jnp.exp(l01) - jnp.exp(l23) + lam_init

    for h in range(H_B):
        ls = slice(h * 128, (h + 1) * 128)
        qn = _rms64(q_ref[0, :, ls], qg_ref[...], half_sum)
        if latent:
            qn = _rope(qn, cq_ref[...], sq_ref[...])
        qn = qn * (DH_B ** -0.5)
        lane = lax.broadcasted_iota(jnp.int32, qn.shape, 1)
        kf = k_scr[:, ls]
        v1 = v_scr[:, h * 256:(h + 1) * 256]

        def unnormalised(qm):
            s = _dot_nt(qm.astype(BF16), kf)
            e = jnp.exp(s - jnp.max(s, axis=-1, keepdims=True)).astype(BF16)
            oz = _dot(e, v1)
            return oz[:, 0:128], oz[:, 128:256]

        o0, z0 = unnormalised(jnp.where(lane < DH_B, qn, 0.0))
        o1, z1 = unnormalised(jnp.where(lane < DH_B, 0.0, qn))
        o = o0 * (1.0 / z0) - o1 * (lam / z1)
        y = o * lax.rsqrt(jnp.mean(o * o, axis=-1, keepdims=True) + EPS) * sg_ref[...]
        y_ref[0, :, ls] = y * (1.0 - lam_init)


def _diff_attn(pb3, layer, qn_g, kn_g, sub_g, lam_p, rope_tab, cache_k, cache_v, kv_out,
               prev_k=(), prev_pb=()):
    b, t, _ = pb3.shape
    latent = rope_tab is not None
    tq = 256
    t_ctx = cache_k.shape[2] if latent else 0
    lam_init = 0.8 - 0.6 * math.exp(-0.3 * layer)
    n_prev = len(prev_k)
    w = H_B * 128

    def small(shape):
        return pl.BlockSpec(shape, lambda bi, qi: (0,) * len(shape))

    seq_blk = lambda c: pl.BlockSpec((1, t, w), lambda bi, qi: (bi, 0, c))
    in_specs = [
        pl.BlockSpec((1, tq, w), lambda bi, qi: (bi, qi, 0)),
        seq_blk(1), seq_blk(2),
        small((1, 128)), small((1, 128)), small((1, 128)), small((4, DH_B)),
    ]
    args = [pb3, pb3, pb3, jnp.tile(qn_g, 2).reshape(1, 128), jnp.tile(kn_g, 2).reshape(1, 128),
            sub_g.reshape(1, 128), lam_p]
    if latent:
        cos_t, sin_t = rope_tab
        in_specs += [
            pl.BlockSpec((tq, 128), lambda bi, qi: (qi, 0)),
            pl.BlockSpec((tq, 128), lambda bi, qi: (qi, 0)),
            small((t, 128)), small((t, 128)),
            pl.BlockSpec((1, 1, t_ctx, w), lambda bi, qi: (bi, layer, 0, 0)),
            pl.BlockSpec((1, 1, t_ctx, w), lambda bi, qi: (bi, layer, 0, 0)),
        ]
        args += [cos_t, sin_t, cos_t, sin_t, cache_k, cache_v]
    in_specs += [seq_blk(0)] * n_prev + [seq_blk(2)] * n_prev
    args += list(prev_k) + list(prev_pb)
    out_specs = [pl.BlockSpec((1, tq, w), lambda bi, qi: (bi, qi, 0))]
    out_shape = [jax.ShapeDtypeStruct((b, t, w), F32)]
    if kv_out == "stack":
        stack_spec = pl.BlockSpec((1, n_prev + 1, t, w), lambda bi, qi: (bi, 0, 0, 0))
        out_specs += [stack_spec, stack_spec]
        out_shape += [jax.ShapeDtypeStruct((b, n_prev + 1, t, w), F32)] * 2
    elif kv_out:
        out_specs.append(seq_blk(0))
        out_shape.append(jax.ShapeDtypeStruct((b, t, w), F32))
    return pl.pallas_call(
        functools.partial(_attn_kernel, t_self=t, t_ctx=t_ctx, lam_init=lam_init,
                          latent=latent, kv_out=kv_out, n_prev=n_prev),
        grid=(b, t // tq),
        in_specs=in_specs,
        out_specs=out_specs,
        out_shape=out_shape,
        scratch_shapes=[pltpu.VMEM((t + t_ctx, w), BF16), pltpu.VMEM((t + t_ctx, 2 * w), BF16)],
        compiler_params=_cparams(("arbitrary", "arbitrary")),
        name="diff_attn",
    )(*args)


def _dft_tables(t):
    def cs(n):
        idx = np.arange(n, dtype=np.int64)
        ang = 2.0 * np.pi * ((idx[:, None] * idx[None, :]) % n).astype(np.float64) / n
        return np.cos(ang), np.sin(ang)

    cc, sc = cs(DG_C)
    ct, st = cs(t)
    chan = jnp.asarray(np.concatenate([cc, sc], axis=1), dtype=F32)
    pos = jnp.asarray(np.concatenate([ct, -st], axis=1), dtype=F32)
    return chan.astype(BF16), pos.astype(BF16)


def _fourier_kernel(u_ref, chan_ref, pos_ref, y_ref, uc_scr, *, t, scale):
    ri = pl.program_id(1)

    @pl.when(ri == 0)
    def _():
        for g in range(G_C):
            ls = slice(g * DG_C, (g + 1) * DG_C)
            u1 = _dot(u_ref[0, :, ls].astype(BF16), chan_ref[...])
            uc_scr[0:t, ls] = u1[:, 0:DG_C].astype(BF16)
            uc_scr[t:2 * t, ls] = u1[:, DG_C:2 * DG_C].astype(BF16)

    y_ref[0] = _dot(pos_ref[...], uc_scr[...]) * scale


def _fourier(pc3):
    b, t, w = pc3.shape
    tr = min(t, 512)
    chan, pos = _dft_tables(t)
    scale = 1.0 / math.sqrt(t * DG_C)
    return pl.pallas_call(
        functools.partial(_fourier_kernel, t=t, scale=scale),
        grid=(b, t // tr),
        in_specs=[
            pl.BlockSpec((1, t, w), lambda bi, ri: (bi, 0, 0)),
            _resident((DG_C, 2 * DG_C)),
            pl.BlockSpec((tr, 2 * t), lambda bi, ri: (ri, 0)),
        ],
        out_specs=pl.BlockSpec((1, tr, w), lambda bi, ri: (bi, ri, 0)),
        out_shape=jax.ShapeDtypeStruct((b, t, w), F32),
        scratch_shapes=[pltpu.VMEM((2 * t, w), BF16)],
        compiler_params=_cparams(("arbitrary", "arbitrary")),
        name="fourier_mix",
    )(pc3, chan, pos)


def _merge_kernel(x_ref, mod_ref, g_ref, oa_ref, ob_ref, oc_ref, wm_ref, wa_ref, wb_ref, wc_ref,
                  wo_ref, o_ref):
    x = x_ref[...]
    h = _modulated_norm(x, g_ref[...], mod_ref)

    def seg(s):
        return _dot(h, wm_ref[:, s * SEG:(s + 1) * SEG])

    ya = (oa_ref[...] * _silu(seg(0))).astype(BF16)
    yb = (ob_ref[...] * _silu(seg(1))).astype(BF16)
    yc = (oc_ref[...] * _silu(seg(2))).astype(BF16)
    parts = []
    for n in range(2):
        cols = slice(n * SEG, (n + 1) * SEG)
        m = jax.nn.sigmoid(seg(3 + n)) * _dot(ya, wa_ref[:, cols])
        m = m + jax.nn.sigmoid(seg(5 + n)) * _dot(yb, wb_ref[:, cols])
        m = m + jax.nn.sigmoid(seg(7 + n)) * _dot(yc, wc_ref[:, cols])
        parts.append(m.astype(BF16))
    merged = jnp.concatenate(parts, axis=1)
    gate = mod_ref[0, 2:3, :]
    o_ref[...] = x + gate * _dot(merged, wo_ref[...])


def _merge(x2, mod3, row_of_tile, norm_g, oa, ob, oc, w_merge, wa, wb, wc, wo):
    n = x2.shape[0]
    tm = TOKEN_TILE
    tok = lambda i: (i, 0)
    in_specs = [
        pl.BlockSpec((tm, D_MODEL), tok),
        pl.BlockSpec((1, 3, D_MODEL), lambda i: (row_of_tile(i), 0, 0)),
        _resident((1, D_MODEL)),
        pl.BlockSpec((tm, SEG), tok), pl.BlockSpec((tm, SEG), tok), pl.BlockSpec((tm, SEG), tok),
        _resident((D_MODEL, len(MERGE_SEGS) * SEG)),
        _resident((SEG, D_MODEL)), _resident((SEG, D_MODEL)), _resident((SEG, D_MODEL)),
        _resident((D_MODEL, D_MODEL)),
    ]
    return pl.pallas_call(
        _merge_kernel,
        grid=(n // tm,),
        in_specs=in_specs,
        out_specs=pl.BlockSpec((tm, D_MODEL), tok),
        out_shape=jax.ShapeDtypeStruct((n, D_MODEL), F32),
        compiler_params=_cparams(("arbitrary",)),
        name="merge_out",
    )(x2, mod3, norm_g.reshape(1, D_MODEL), oa, ob, oc, w_merge, wa, wb, wc, wo)


def _rope_tables(t):
    nf = DH_B // 4
    n_rows = t // GRID_W
    pos_row = jnp.repeat(jnp.arange(n_rows, dtype=F32), GRID_W)
    pos_col = jnp.tile(jnp.arange(GRID_W, dtype=F32), n_rows)
    inv = ROPE_BASE ** (-jnp.arange(nf, dtype=F32) / nf)
    ar = pos_row[:, None] * inv[None, :]
    ac = pos_col[:, None] * inv[None, :]
    cos64 = jnp.concatenate([jnp.cos(ar), jnp.cos(ar), jnp.cos(ac), jnp.cos(ac)], axis=1)
    sin64 = jnp.concatenate([-jnp.sin(ar), jnp.sin(ar), -jnp.sin(ac), jnp.sin(ac)], axis=1)
    return jnp.tile(cos64, (1, 2)), jnp.tile(sin64, (1, 2))


def _take_segments(w, segs):
    return jnp.concatenate([w[:, s * SEG:(s + 1) * SEG] for s in segs], axis=1).astype(BF16)


def _layer(x, mod3, row_of_tile, layer, lw, log_lb, log1m_lb, ctx=None, rope_tab=None, prev=None):
    (norm_g, w_proj, w_merge, hgrn_g, qn_g, kn_g, lam_p, sub_g, wa, wb, wc, wo) = lw
    b, t, _ = x.shape
    x2 = x.reshape(b * t, D_MODEL)
    pa, pb, pc = _in_proj(x2, mod3, row_of_tile, norm_g, w_proj, log_lb, log1m_lb)
    pa3 = pa.reshape(b, t, 4 * SEG)
    pb3 = pb.reshape(b, t, 3 * SEG)
    is_ctx = ctx is None
    if not is_ctx:
        oa, _ = _hgrn(pa3, hgrn_g, ctx[2], want_state=False)
        (ob,) = _diff_attn(pb3, layer, qn_g, kn_g, sub_g, lam_p, rope_tab, ctx[0], ctx[1], None)
    elif prev is None:
        oa, s_out = _hgrn(pa3, hgrn_g, None, want_state=True)
        ob, k_out = _diff_attn(pb3, layer, qn_g, kn_g, sub_g, lam_p, None, None, None, "plain")
        v_out = pb3
    else:
        oa, s_out = _hgrn(pa3, hgrn_g, None, want_state=True, prev_states=prev[2])
        ob, k_out, v_out = _diff_attn(pb3, layer, qn_g, kn_g, sub_g, lam_p, None, None, None,
                                      "stack", prev_k=prev[0], prev_pb=prev[1])
    oc = _fourier(pc.reshape(b, t, SEG))
    x_new = _merge(x2, mod3, row_of_tile, norm_g, oa.reshape(b * t, SEG), ob.reshape(b * t, SEG),
                   oc.reshape(b * t, SEG), w_merge, wa, wb, wc, wo)
    x_new = x_new.reshape(b, t, D_MODEL)
    if is_ctx:
        return x_new, k_out, v_out, s_out
    return x_new


def kernel(x_prompt, x_sample, c, cache_diff_k, cache_diff_v, state_hgrn, c_ctx, ada_w, ada_b,
           norm_g, w_in, hgrn_lb, hgrn_norm_g, diff_qn_g, diff_kn_g, diff_lambda, diff_subln_g,
           w_branch_a, w_branch_b, w_branch_c, w_out):
    batch, seq, _ = x_prompt.shape
    dec_b, dec_t, _ = x_sample.shape
    past = cache_diff_k.shape[2]

    lb_all = jnp.cumsum(jax.nn.softmax(hgrn_lb.astype(F32), axis=0), axis=0)
    lb_all = lb_all - lb_all[0]
    log_lb = jnp.log(lb_all)
    log1m_lb = jnp.log1p(-lb_all)

    cond8 = jnp.zeros((8, D_MODEL), F32).at[0].set(c_ctx).at[1:1 + dec_b].set(c)
    mod = _modulation(cond8, ada_w, ada_b).reshape(DEPTH, 8, 3, D_MODEL)

    rope_tab = _rope_tables(dec_t)
    cache_k4 = cache_diff_k.reshape(dec_b, DEPTH, past, H_B * 2 * DH_B)
    cache_v4 = cache_diff_v.reshape(dec_b, DEPTH, past, H_B * DV_B)
    tiles_per_seq = dec_t // TOKEN_TILE
    ctx_row = lambda i: 0
    lat_row = lambda i: 1 + i // tiles_per_seq

    xp, xs = x_prompt, x_sample
    prev_k, prev_pb, prev_s = [], [], []
    for l in range(DEPTH):
        lw = (norm_g[l], _take_segments(w_in[l], PROJ_SEGS), _take_segments(w_in[l], MERGE_SEGS),
              hgrn_norm_g[l], diff_qn_g[l], diff_kn_g[l], diff_lambda[l], diff_subln_g[l],
              w_branch_a[l].astype(BF16), w_branch_b[l].astype(BF16),
              w_branch_c[l].astype(BF16), w_out[l].astype(BF16))
        mod3 = mod[l]
        last = l == DEPTH - 1
        xp, k_c, v_c, s_c = _layer(xp, mod3, ctx_row, l, lw, log_lb[l], log1m_lb[l],
                                   prev=(prev_k, prev_pb, prev_s) if last and l > 0 else None)
        if not last:
            prev_k.append(k_c)
            prev_pb.append(v_c)
            prev_s.append(s_c)
        xs = _layer(xs, mod3, lat_row, l, lw, log_lb[l], log1m_lb[l],
                    ctx=(cache_k4, cache_v4, state_hgrn[:, l]), rope_tab=rope_tab)
    new_k = k_c.reshape(batch, DEPTH, seq, H_B, 2, DH_B)
    new_v = v_c.reshape(batch, DEPTH, seq, H_B, DV_B)
    return (xp, xs, new_k, new_v, s_c)
```

```python
import functools
import math

import numpy as np
import jax
import jax.numpy as jnp
from jax import lax
from jax.experimental import pallas as pl
from jax.experimental.pallas import tpu as pltpu

F32 = jnp.float32
BF16 = jnp.bfloat16

D_MODEL = 1024
DEPTH = 2
GRID_W = 64
EPS = 1e-6
H_A = 4
DK_A = 128
DV_A = 128
H_B = 4
DH_B = 64
DV_B = 128
G_C = 4
DG_C = 128
ROPE_BASE = 10000.0
SEG = 512
(S_QA, S_FFA, S_FBA, S_IA, S_ZA, S_QB, S_KB, S_VB, S_ZB, S_UC, S_ZC) = range(11)
S_GATE = 11
PROJ_SEGS = (S_QA, S_FFA, S_FBA, S_IA, S_QB, S_KB, S_VB, S_UC)
MERGE_SEGS = (S_ZA, S_ZB, S_ZC) + tuple(range(S_GATE, S_GATE + 6))

CHUNK = 128
SUB = 16
HEADS_PER_STEP = 4
TOKEN_TILE = 512
VMEM_LIMIT = 56 * 1024 * 1024


def _cparams(sem):
    return pltpu.CompilerParams(dimension_semantics=sem, vmem_limit_bytes=VMEM_LIMIT)


def _resident(shape):
    return pl.BlockSpec(shape, lambda *_: (0,) * len(shape), pipeline_mode=pl.Buffered(1))


def _silu(x):
    return x * jax.nn.sigmoid(x)


def _log_sigmoid(x):
    return jnp.minimum(x, 0.0) - jnp.log1p(jnp.exp(-jnp.abs(x)))


def _dot(a, b):
    return jnp.dot(a, b, preferred_element_type=F32)


def _dot_nt(a, b):
    return lax.dot_general(a, b, (((1,), (1,)), ((), ())), preferred_element_type=F32)


def _modulated_norm(x, g, mod_ref):
    y = x * lax.rsqrt(jnp.mean(x * x, axis=-1, keepdims=True) + EPS) * g
    shift = mod_ref[0, 0:1, :]
    scale = mod_ref[0, 1:2, :]
    return (y * (1.0 + scale) + shift).astype(BF16)


def _mod_kernel(c_ref, w_ref, b_ref, o_ref):
    c = c_ref[...]
    s = _silu(c).astype(BF16)
    o_ref[0] = _dot(s, w_ref[0].astype(BF16)) + b_ref[0]


def _modulation(cond8, ada_w, ada_b):
    tn = 512
    return pl.pallas_call(
        _mod_kernel,
        grid=(DEPTH, 3 * D_MODEL // tn),
        in_specs=[
            pl.BlockSpec((8, D_MODEL), lambda l, j: (0, 0)),
            pl.BlockSpec((1, D_MODEL, tn), lambda l, j: (l, 0, j)),
            pl.BlockSpec((1, 1, tn), lambda l, j: (l, 0, j)),
        ],
        out_specs=pl.BlockSpec((1, 8, tn), lambda l, j: (l, 0, j)),
        out_shape=jax.ShapeDtypeStruct((DEPTH, 8, 3 * D_MODEL), F32),
        compiler_params=_cparams(("arbitrary", "arbitrary")),
        name="adaln_mod",
    )(cond8, ada_w, ada_b.reshape(DEPTH, 1, 3 * D_MODEL))


def _proj_kernel(x_ref, mod_ref, g_ref, w_ref, la_ref, l1_ref, pa_ref, pb_ref, pc_ref):
    h = _modulated_norm(x_ref[...], g_ref[...], mod_ref)

    def seg(s):
        return _dot(h, w_ref[:, s * SEG:(s + 1) * SEG])

    def log_f(raw, d):
        a = la_ref[d:d + 1, :]
        b = l1_ref[d:d + 1, :] + _log_sigmoid(raw)
        return jnp.maximum(a, b) + jnp.log1p(jnp.exp(-jnp.abs(a - b)))

    pa_ref[:, 0 * SEG:1 * SEG] = _silu(seg(0))
    pa_ref[:, 1 * SEG:2 * SEG] = log_f(seg(1), 0)
    pa_ref[:, 2 * SEG:3 * SEG] = log_f(seg(2), 1)
    pa_ref[:, 3 * SEG:4 * SEG] = seg(3)
    for s in range(3):
        pb_ref[:, s * SEG:(s + 1) * SEG] = seg(4 + s)
    pc_ref[...] = seg(7)


def _in_proj(x2, mod3, row_of_tile, norm_g, w_proj, log_lb, log1m_lb):
    n = x2.shape[0]
    tm = TOKEN_TILE
    tok = lambda i: (i, 0)
    return pl.pallas_call(
        _proj_kernel,
        grid=(n // tm,),
        in_specs=[
            pl.BlockSpec((tm, D_MODEL), tok),
            pl.BlockSpec((1, 3, D_MODEL), lambda i: (row_of_tile(i), 0, 0)),
            _resident((1, D_MODEL)),
            _resident((D_MODEL, len(PROJ_SEGS) * SEG)),
            _resident((2, SEG)),
            _resident((2, SEG)),
        ],
        out_specs=[pl.BlockSpec((tm, 4 * SEG), tok), pl.BlockSpec((tm, 3 * SEG), tok),
                   pl.BlockSpec((tm, SEG), tok)],
        out_shape=[jax.ShapeDtypeStruct((n, 4 * SEG), F32), jax.ShapeDtypeStruct((n, 3 * SEG), F32),
                   jax.ShapeDtypeStruct((n, SEG), F32)],
        compiler_params=_cparams(("arbitrary",)),
        name="in_proj",
    )(x2, mod3, norm_g.reshape(1, D_MODEL), w_proj, log_lb, log1m_lb)


def _split3(x):
    hi = x.astype(BF16)
    r = x - hi.astype(F32)
    mid = r.astype(BF16)
    lo = (r - mid.astype(F32)).astype(BF16)
    return hi, mid, lo


def _level_factors(q, k, g, m, reverse, diag):
    nb = CHUNK // m
    fq, fk = [], []
    zeros = jnp.zeros((m, 128), BF16)
    for i in range(nb):
        rows = slice(i * m, (i + 1) * m)
        sl = g[rows]

        def scaled(x, e):
            return (x[rows] * jnp.exp(e)).astype(BF16)

        if diag:
            r = i * m + (m // 2 - 1 if not reverse else m // 2)
            ref = g[r:r + 1]
            fq.append(scaled(q, sl - ref))
            fk.append(scaled(k, ref - sl))
        elif (i % 2 == 1) != reverse:
            r = (i + 1) * m if reverse else i * m - 1
            fq.append(scaled(q, sl - g[r:r + 1]))
            fk.append(zeros)
        else:
            r = i * m if reverse else (i + 1) * m - 1
            fq.append(zeros)
            fk.append(scaled(k, g[r:r + 1] - sl))
    return jnp.concatenate(fq, axis=0), jnp.concatenate(fk, axis=0)


def _level_masks(reverse):
    t = lax.broadcasted_iota(jnp.int32, (CHUNK, CHUNK), 0)
    s = lax.broadcasted_iota(jnp.int32, (CHUNK, CHUNK), 1)
    sh = int(math.log2(SUB))
    same = (t >> sh) == (s >> sh)
    masks = [same & ((s >= t) if reverse else (s <= t))]
    m = SUB
    while m < CHUNK:
        sh = int(math.log2(m))
        tb, sb = t >> sh, s >> sh
        if not reverse:
            masks.append(((tb & 1) == 1) & (sb == tb - 1))
        else:
            masks.append(((tb & 1) == 0) & (sb == tb + 1))
        m *= 2
    return masks


def _hgrn_chunk(lf, q, v, st, tri, masks, reverse):
    end_row = 0 if reverse else CHUNK - 1
    k = 1.0 - jnp.exp(lf)
    g = _dot(tri, jnp.concatenate(_split3(lf), axis=0))
    g_end = g[end_row:end_row + 1]
    vt_bf = v.T.astype(BF16)

    a = jnp.zeros((CHUNK, CHUNK), F32)
    m = SUB
    for lvl, mask in enumerate(masks):
        fq, fk = _level_factors(q, k, g, m, reverse, diag=(lvl == 0))
        a = jnp.where(mask, _dot_nt(fq, fk), a)
        if lvl > 0:
            m *= 2

    qg = (q * jnp.exp(g)).astype(BF16)
    o = _dot_nt(jnp.concatenate([qg, a.astype(BF16)], axis=1),
                jnp.concatenate([st.astype(BF16), vt_bf], axis=1))

    kg = (k * jnp.exp(g_end - g)).astype(BF16)
    u_t = _dot(vt_bf, kg)
    return o, jnp.exp(g_end) * st + u_t


def _hgrn_kernel(*refs, n_chunks, has_state, want_state, n_prev):
    it = iter(refs)
    q_ref, lff_ref, lfb_ref, v_ref, g_ref = (next(it) for _ in range(5))
    s0_ref = next(it) if has_state else None
    prev_refs = [next(it) for _ in range(n_prev)]
    y_ref = next(it)
    sfin_ref = next(it) if want_state else None
    o_scr, st_scr = next(it), next(it)
    hpb = HEADS_PER_STEP

    t_i = lax.broadcasted_iota(jnp.int32, (CHUNK, CHUNK), 0)
    s_i = lax.broadcasted_iota(jnp.int32, (CHUNK, CHUNK), 1)
    tris = [jnp.where(s_i <= t_i, 1.0, 0.0).astype(BF16), jnp.where(s_i >= t_i, 1.0, 0.0).astype(BF16)]
    tris = [jnp.concatenate([tr, tr, tr], axis=1) for tr in tris]
    masks = [_level_masks(False), _level_masks(True)]
    lf_refs = (lff_ref, lfb_ref)

    for hh in range(hpb):
        for d in range(2):
            if has_state:
                st_scr[2 * hh + d] = s0_ref[0, d, hh].T
            else:
                st_scr[2 * hh + d] = jnp.zeros((DV_A, DK_A), F32)

    def step(ci):
        for hh in range(hpb):
            ls = slice(hh * 128, (hh + 1) * 128)
            for d in range(2):
                c = ci if d == 0 else n_chunks - 1 - ci
                r0 = c * CHUNK
                if not isinstance(r0, int):
                    r0 = pl.multiple_of(r0, CHUNK)
                rows = pl.ds(r0, CHUNK)
                o, st = _hgrn_chunk(lf_refs[d][0, rows, ls], q_ref[0, rows, ls], v_ref[0, rows, ls],
                                    st_scr[2 * hh + d], tris[d], masks[d], reverse=(d == 1))
                o_scr[d, rows, ls] = o
                st_scr[2 * hh + d] = st

    if n_chunks <= 2:
        for ci in range(n_chunks):
            step(ci)
    else:
        def body(i, carry):
            step(2 * i)
            step(2 * i + 1)
            return carry
        lax.fori_loop(0, n_chunks // 2, body, 0)

    for j, p_ref in enumerate(prev_refs):
        sfin_ref[0, j] = p_ref[0]
    for hh in range(hpb):
        ls = slice(hh * 128, (hh + 1) * 128)
        if want_state:
            for d in range(2):
                if n_prev:
                    sfin_ref[0, n_prev, d, hh] = st_scr[2 * hh + d].T
                else:
                    sfin_ref[0, d, hh] = st_scr[2 * hh + d].T
        o = o_scr[0, :, ls] + o_scr[1, :, ls]
        y_ref[0, :, ls] = o * lax.rsqrt(jnp.mean(o * o, axis=-1, keepdims=True) + EPS) * g_ref[0, :, ls]


def _hgrn(pa3, hgrn_g, s0, want_state, prev_states=()):
    b, t, _ = pa3.shape
    n_chunks = t // CHUNK
    has_state = s0 is not None
    hpb = HEADS_PER_STEP
    w = 128 * hpb
    per_seg = SEG // w
    n_prev = len(prev_states)

    def col(seg):
        return pl.BlockSpec((1, t, w), lambda bi, hp: (bi, 0, seg * per_seg + hp))

    in_specs = [col(0), col(1), col(2), col(3),
                pl.BlockSpec((1, 1, w), lambda bi, hp: (hp, 0, 0))]
    args = [pa3, pa3, pa3, pa3, hgrn_g.reshape(H_A // hpb, 1, w)]
    st_spec = pl.BlockSpec((1, 2, hpb, DK_A, DV_A), lambda bi, hp: (bi, 0, hp, 0, 0))
    if has_state:
        in_specs.append(st_spec)
        args.append(s0)
    in_specs += [st_spec] * n_prev
    args += list(prev_states)
    out_specs = [pl.BlockSpec((1, t, w), lambda bi, hp: (bi, 0, hp))]
    out_shape = [jax.ShapeDtypeStruct((b, t, H_A * DV_A), F32)]
    if want_state and n_prev:
        out_specs.append(pl.BlockSpec((1, n_prev + 1, 2, hpb, DK_A, DV_A),
                                      lambda bi, hp: (bi, 0, 0, hp, 0, 0)))
        out_shape.append(jax.ShapeDtypeStruct((b, n_prev + 1, 2, H_A, DK_A, DV_A), F32))
    elif want_state:
        out_specs.append(st_spec)
        out_shape.append(jax.ShapeDtypeStruct((b, 2, H_A, DK_A, DV_A), F32))
    res = pl.pallas_call(
        functools.partial(_hgrn_kernel, n_chunks=n_chunks, has_state=has_state,
                          want_state=want_state, n_prev=n_prev),
        grid=(b, H_A // hpb),
        in_specs=in_specs,
        out_specs=out_specs,
        out_shape=out_shape,
        scratch_shapes=[pltpu.VMEM((2, t, w), F32), pltpu.VMEM((2 * hpb, DV_A, DK_A), F32)],
        compiler_params=_cparams(("arbitrary", "arbitrary")),
        name="hgrn2",
    )(*args)
    return res if want_state else (res[0], None)


def _half_sum_matrix():
    r = lax.broadcasted_iota(jnp.int32, (128, 128), 0)
    c = lax.broadcasted_iota(jnp.int32, (128, 128), 1)
    return jnp.where((r >> 6) == (c >> 6), 1.0, 0.0).astype(BF16)


def _rms64(x, g, half_sum):
    x2 = x * x
    hi = x2.astype(BF16)
    lo = (x2 - hi.astype(F32)).astype(BF16)
    ss = _dot(hi, half_sum) + _dot(lo, half_sum)
    return x * lax.rsqrt(ss / DH_B + EPS) * g


def _rope(x, cos, sin):
    lane = lax.broadcasted_iota(jnp.int32, x.shape, 1)
    first = (lane & 31) < 16
    partner = jnp.where(first, pltpu.roll(x, 112, 1), pltpu.roll(x, 16, 1))
    return x * cos + partner * sin


def _attn_kernel(*refs, t_self, t_ctx, lam_init, latent, kv_out, n_prev):
    it = iter(refs)
    q_ref, k_ref, v_ref = (next(it) for _ in range(3))
    qg_ref, kg_ref, sg_ref, lam_ref = (next(it) for _ in range(4))
    if latent:
        cq_ref, sq_ref, ck_ref, sk_ref, kc_ref, vc_ref = (next(it) for _ in range(6))
    prev_k = [next(it) for _ in range(n_prev)]
    prev_v = [next(it) for _ in range(n_prev)]
    y_ref = next(it)
    kout_ref = next(it) if kv_out else None
    vout_ref = next(it) if kv_out == "stack" else None
    k_scr, v_scr = next(it), next(it)

    qi = pl.program_id(1)
    half_sum = _half_sum_matrix()
    t_all = t_self + t_ctx

    @pl.when(qi == 0)
    def _():
        for j in range(n_prev):
            kout_ref[0, j] = prev_k[j][0]
            vout_ref[0, j] = prev_v[j][0]
        if kv_out == "stack":
            vout_ref[0, n_prev] = v_ref[0]
        for h in range(H_B):
            ls = slice(h * 128, (h + 1) * 128)
            vs = slice(h * 256, h * 256 + 128)
            kn = _rms64(k_ref[0, :, ls], kg_ref[...], half_sum)
            if kv_out == "stack":
                kout_ref[0, n_prev, :, ls] = kn
            elif kv_out:
                kout_ref[0, :, ls] = kn
            if latent:
                kn = _rope(kn, ck_ref[...], sk_ref[...])
                k_scr[t_self:t_all, ls] = kc_ref[0, 0, :, ls].astype(BF16)
                v_scr[t_self:t_all, vs] = vc_ref[0, 0, :, ls].astype(BF16)
            k_scr[0:t_self, ls] = kn.astype(BF16)
            v_scr[0:t_self, vs] = v_ref[0, :, ls].astype(BF16)
            v_scr[:, h * 256 + 128:(h + 1) * 256] = jnp.ones((t_all, 128), BF16)

    lp = lam_ref[...]
    l01 = jnp.sum(lp[0:1] * lp[1:2], axis=-1, keepdims=True)
    l23 = jnp.sum(lp[2:3] * lp[3:4], axis=-1, keepdims=True)
    lam = jnp.exp(l01) - jnp.exp(l23) + lam_init

    for h in range(H_B):
        ls = slice(h * 128, (h + 1) * 128)
        qn = _rms64(q_ref[0, :, ls], qg_ref[...], half_sum)
        if latent:
            qn = _rope(qn, cq_ref[...], sq_ref[...])
        qn = qn * (DH_B ** -0.5)
        lane = lax.broadcasted_iota(jnp.int32, qn.shape, 1)
        kf = k_scr[:, ls]
        v1 = v_scr[:, h * 256:(h + 1) * 256]

        def unnormalised(qm):
            s = _dot_nt(qm.astype(BF16), kf)
            e = jnp.exp(s - jnp.max(s, axis=-1, keepdims=True)).astype(BF16)
            oz = _dot(e, v1)
            return oz[:, 0:128], oz[:, 128:256]

        o0, z0 = unnormalised(jnp.where(lane < DH_B, qn, 0.0))
        o1, z1 = unnormalised(jnp.where(lane < DH_B, 0.0, qn))
        o = o0 * (1.0 / z0) - o1 * (lam / z1)
        y = o * lax.rsqrt(jnp.mean(o * o, axis=-1, keepdims=True) + EPS) * sg_ref[...]
        y_ref[0, :, ls] = y * (1.0 - lam_init)


def _diff_attn(pb3, layer, qn_g, kn_g, sub_g, lam_p, rope_tab, cache_k, cache_v, kv_out,
               prev_k=(), prev_pb=()):
    b, t, _ = pb3.shape
    latent = rope_tab is not None
    tq = 256
    t_ctx = cache_k.shape[2] if latent else 0
    lam_init = 0.8 - 0.6 * math.exp(-0.3 * layer)
    n_prev = len(prev_k)
    w = H_B * 128

    def small(shape):
        return pl.BlockSpec(shape, lambda bi, qi: (0,) * len(shape))

    seq_blk = lambda c: pl.BlockSpec((1, t, w), lambda bi, qi: (bi, 0, c))
    in_specs = [
        pl.BlockSpec((1, tq, w), lambda bi, qi: (bi, qi, 0)),
        seq_blk(1), seq_blk(2),
        small((1, 128)), small((1, 128)), small((1, 128)), small((4, DH_B)),
    ]
    args = [pb3, pb3, pb3, jnp.tile(qn_g, 2).reshape(1, 128), jnp.tile(kn_g, 2).reshape(1, 128),
            sub_g.reshape(1, 128), lam_p]
    if latent:
        cos_t, sin_t = rope_tab
        in_specs += [
            pl.BlockSpec((tq, 128), lambda bi, qi: (qi, 0)),
            pl.BlockSpec((tq, 128), lambda bi, qi: (qi, 0)),
            small((t, 128)), small((t, 128)),
            pl.BlockSpec((1, 1, t_ctx, w), lambda bi, qi: (bi, layer, 0, 0)),
            pl.BlockSpec((1, 1, t_ctx, w), lambda bi, qi: (bi, layer, 0, 0)),
        ]
        args += [cos_t, sin_t, cos_t, sin_t, cache_k, cache_v]
    in_specs += [seq_blk(0)] * n_prev + [seq_blk(2)] * n_prev
    args += list(prev_k) + list(prev_pb)
    out_specs = [pl.BlockSpec((1, tq, w), lambda bi, qi: (bi, qi, 0))]
    out_shape = [jax.ShapeDtypeStruct((b, t, w), F32)]
    if kv_out == "stack":
        stack_spec = pl.BlockSpec((1, n_prev + 1, t, w), lambda bi, qi: (bi, 0, 0, 0))
        out_specs += [stack_spec, stack_spec]
        out_shape += [jax.ShapeDtypeStruct((b, n_prev + 1, t, w), F32)] * 2
    elif kv_out:
        out_specs.append(seq_blk(0))
        out_shape.append(jax.ShapeDtypeStruct((b, t, w), F32))
    return pl.pallas_call(
        functools.partial(_attn_kernel, t_self=t, t_ctx=t_ctx, lam_init=lam_init,
                          latent=latent, kv_out=kv_out, n_prev=n_prev),
        grid=(b, t // tq),
        in_specs=in_specs,
        out_specs=out_specs,
        out_shape=out_shape,
        scratch_shapes=[pltpu.VMEM((t + t_ctx, w), BF16), pltpu.VMEM((t + t_ctx, 2 * w), BF16)],
        compiler_params=_cparams(("arbitrary", "arbitrary")),
        name="diff_attn",
    )(*args)


def _dft_tables(t):
    def cs(n):
        idx = np.arange(n, dtype=np.int64)
        ang = 2.0 * np.pi * ((idx[:, None] * idx[None, :]) % n).astype(np.float64) / n
        return np.cos(ang), np.sin(ang)

    cc, sc = cs(DG_C)
    ct, st = cs(t)
    chan = jnp.asarray(np.concatenate([cc, sc], axis=1), dtype=F32)
    pos = jnp.asarray(np.concatenate([ct, -st], axis=1), dtype=F32)
    return chan.astype(BF16), pos.astype(BF16)


def _fourier_kernel(u_ref, chan_ref, pos_ref, y_ref, uc_scr, *, t, scale):
    ri = pl.program_id(1)

    @pl.when(ri == 0)
    def _():
        for g in range(G_C):
            ls = slice(g * DG_C, (g + 1) * DG_C)
            u1 = _dot(u_ref[0, :, ls].astype(BF16), chan_ref[...])
            uc_scr[0:t, ls] = u1[:, 0:DG_C].astype(BF16)
            uc_scr[t:2 * t, ls] = u1[:, DG_C:2 * DG_C].astype(BF16)

    y_ref[0] = _dot(pos_ref[...], uc_scr[...]) * scale


def _fourier(pc3):
    b, t, w = pc3.shape
    tr = min(t, 512)
    chan, pos = _dft_tables(t)
    scale = 1.0 / math.sqrt(t * DG_C)
    return pl.pallas_call(
        functools.partial(_fourier_kernel, t=t, scale=scale),
        grid=(b, t // tr),
        in_specs=[
            pl.BlockSpec((1, t, w), lambda bi, ri: (bi, 0, 0)),
            _resident((DG_C, 2 * DG_C)),
            pl.BlockSpec((tr, 2 * t), lambda bi, ri: (ri, 0)),
        ],
        out_specs=pl.BlockSpec((1, tr, w), lambda bi, ri: (bi, ri, 0)),
        out_shape=jax.ShapeDtypeStruct((b, t, w), F32),
        scratch_shapes=[pltpu.VMEM((2 * t, w), BF16)],
        compiler_params=_cparams(("arbitrary", "arbitrary")),
        name="fourier_mix",
    )(pc3, chan, pos)


def _merge_kernel(x_ref, mod_ref, g_ref, oa_ref, ob_ref, oc_ref, wm_ref, wa_ref, wb_ref, wc_ref,
                  wo_ref, o_ref):
    x = x_ref[...]
    h = _modulated_norm(x, g_ref[...], mod_ref)

    def seg(s):
        return _dot(h, wm_ref[:, s * SEG:(s + 1) * SEG])

    ya = (oa_ref[...] * _silu(seg(0))).astype(BF16)
    yb = (ob_ref[...] * _silu(seg(1))).astype(BF16)
    yc = (oc_ref[...] * _silu(seg(2))).astype(BF16)
    parts = []
    for n in range(2):
        cols = slice(n * SEG, (n + 1) * SEG)
        m = jax.nn.sigmoid(seg(3 + n)) * _dot(ya, wa_ref[:, cols])
        m = m + jax.nn.sigmoid(seg(5 + n)) * _dot(yb, wb_ref[:, cols])
        m = m + jax.nn.sigmoid(seg(7 + n)) * _dot(yc, wc_ref[:, cols])
        parts.append(m.astype(BF16))
    merged = jnp.concatenate(parts, axis=1)
    gate = mod_ref[0, 2:3, :]
    o_ref[...] = x + gate * _dot(merged, wo_ref[...])


def _merge(x2, mod3, row_of_tile, norm_g, oa, ob, oc, w_merge, wa, wb, wc, wo):
    n = x2.shape[0]
    tm = TOKEN_TILE
    tok = lambda i: (i, 0)
    in_specs = [
        pl.BlockSpec((tm, D_MODEL), tok),
        pl.BlockSpec((1, 3, D_MODEL), lambda i: (row_of_tile(i), 0, 0)),
        _resident((1, D_MODEL)),
        pl.BlockSpec((tm, SEG), tok), pl.BlockSpec((tm, SEG), tok), pl.BlockSpec((tm, SEG), tok),
        _resident((D_MODEL, len(MERGE_SEGS) * SEG)),
        _resident((SEG, D_MODEL)), _resident((SEG, D_MODEL)), _resident((SEG, D_MODEL)),
        _resident((D_MODEL, D_MODEL)),
    ]
    return pl.pallas_call(
        _merge_kernel,
        grid=(n // tm,),
        in_specs=in_specs,
        out_specs=pl.BlockSpec((tm, D_MODEL), tok),
        out_shape=jax.ShapeDtypeStruct((n, D_MODEL), F32),
        compiler_params=_cparams(("arbitrary",)),
        name="merge_out",
    )(x2, mod3, norm_g.reshape(1, D_MODEL), oa, ob, oc, w_merge, wa, wb, wc, wo)


def _rope_tables(t):
    nf = DH_B // 4
    n_rows = t // GRID_W
    pos_row = jnp.repeat(jnp.arange(n_rows, dtype=F32), GRID_W)
    pos_col = jnp.tile(jnp.arange(GRID_W, dtype=F32), n_rows)
    inv = ROPE_BASE ** (-jnp.arange(nf, dtype=F32) / nf)
    ar = pos_row[:, None] * inv[None, :]
    ac = pos_col[:, None] * inv[None, :]
    cos64 = jnp.concatenate([jnp.cos(ar), jnp.cos(ar), jnp.cos(ac), jnp.cos(ac)], axis=1)
    sin64 = jnp.concatenate([-jnp.sin(ar), jnp.sin(ar), -jnp.sin(ac), jnp.sin(ac)], axis=1)
    return jnp.tile(cos64, (1, 2)), jnp.tile(sin64, (1, 2))


def _take_segments(w, segs):
    return jnp.concatenate([w[:, s * SEG:(s + 1) * SEG] for s in segs], axis=1).astype(BF16)


def _layer(x, mod3, row_of_tile, layer, lw, log_lb, log1m_lb, ctx=None, rope_tab=None, prev=None):
    (norm_g, w_proj, w_merge, hgrn_g, qn_g, kn_g, lam_p, sub_g, wa, wb, wc, wo) = lw
    b, t, _ = x.shape
    x2 = x.reshape(b * t, D_MODEL)
    pa, pb, pc = _in_proj(x2, mod3, row_of_tile, norm_g, w_proj, log_lb, log1m_lb)
    pa3 = pa.reshape(b, t, 4 * SEG)
    pb3 = pb.reshape(b, t, 3 * SEG)
    is_ctx = ctx is None
    if not is_ctx:
        oa, _ = _hgrn(pa3, hgrn_g, ctx[2], want_state=False)
        (ob,) = _diff_attn(pb3, layer, qn_g, kn_g, sub_g, lam_p, rope_tab, ctx[0], ctx[1], None)
    elif prev is None:
        oa, s_out = _hgrn(pa3, hgrn_g, None, want_state=True)
        ob, k_out = _diff_attn(pb3, layer, qn_g, kn_g, sub_g, lam_p, None, None, None, "plain")
        v_out = pb3
    else:
        oa, s_out = _hgrn(pa3, hgrn_g, None, want_state=True, prev_states=prev[2])
        ob, k_out, v_out = _diff_attn(pb3, layer, qn_g, kn_g, sub_g, lam_p, None, None, None,
                                      "stack", prev_k=prev[0], prev_pb=prev[1])
    oc = _fourier(pc.reshape(b, t, SEG))
    x_new = _merge(x2, mod3, row_of_tile, norm_g, oa.reshape(b * t, SEG), ob.reshape(b * t, SEG),
                   oc.reshape(b * t, SEG), w_merge, wa, wb, wc, wo)
    x_new = x_new.reshape(b, t, D_MODEL)
    if is_ctx:
        return x_new, k_out, v_out, s_out
    return x_new


def kernel(x_prompt, x_sample, c, cache_diff_k, cache_diff_v, state_hgrn, c_ctx, ada_w, ada_b,
           norm_g, w_in, hgrn_lb, hgrn_norm_g, diff_qn_g, diff_kn_g, diff_lambda, diff_subln_g,
           w_branch_a, w_branch_b, w_branch_c, w_out):
    batch, seq, _ = x_prompt.shape
    dec_b, dec_t, _ = x_sample.shape
    past = cache_diff_k.shape[2]

    lb_all = jnp.cumsum(jax.nn.softmax(hgrn_lb.astype(F32), axis=0), axis=0)
    lb_all = lb_all - lb_all[0]
    log_lb = jnp.log(lb_all)
    log1m_lb = jnp.log1p(-lb_all)

    cond8 = jnp.zeros((8, D_MODEL), F32).at[0].set(c_ctx).at[1:1 + dec_b].set(c)
    mod = _modulation(cond8, ada_w, ada_b).reshape(DEPTH, 8, 3, D_MODEL)

    rope_tab = _rope_tables(dec_t)
    cache_k4 = cache_diff_k.reshape(dec_b, DEPTH, past, H_B * 2 * DH_B)
    cache_v4 = cache_diff_v.reshape(dec_b, DEPTH, past, H_B * DV_B)
    tiles_per_seq = dec_t // TOKEN_TILE
    ctx_row = lambda i: 0
    lat_row = lambda i: 1 + i // tiles_per_seq

    xp, xs = x_prompt, x_sample
    prev_k, prev_pb, prev_s = [], [], []
    for l in range(DEPTH):
        lw = (norm_g[l], _take_segments(w_in[l], PROJ_SEGS), _take_segments(w_in[l], MERGE_SEGS),
              hgrn_norm_g[l], diff_qn_g[l], diff_kn_g[l], diff_lambda[l], diff_subln_g[l],
              w_branch_a[l].astype(BF16), w_branch_b[l].astype(BF16),
              w_branch_c[l].astype(BF16), w_out[l].astype(BF16))
        mod3 = mod[l]
        last = l == DEPTH - 1
        xp, k_c, v_c, s_c = _layer(xp, mod3, ctx_row, l, lw, log_lb[l], log1m_lb[l],
                                   prev=(prev_k, prev_pb, prev_s) if last and l > 0 else None)
        if not last:
            prev_k.append(k_c)
            prev_pb.append(v_c)
            prev_s.append(s_c)
        xs = _layer(xs, mod3, lat_row, l, lw, log_lb[l], log1m_lb[l],
                    ctx=(cache_k4, cache_v4, state_hgrn[:, l]), rope_tab=rope_tab)
    new_k = k_c.reshape(batch, DEPTH, seq, H_B, 2, DH_B)
    new_v = v_c.reshape(batch, DEPTH, seq, H_B, DV_B)
    return (xp, xs, new_k, new_v, s_c)
```

```python
import functools
import math

import numpy as np
import jax
import jax.numpy as jnp
from jax import lax
from jax.experimental import pallas as pl
from jax.experimental.pallas import tpu as pltpu

F32 = jnp.float32
BF16 = jnp.bfloat16

D_MODEL = 1024
DEPTH = 2
GRID_W = 64
EPS = 1e-6
H_A = 4
DK_A = 128
DV_A = 128
H_B = 4
DH_B = 64
DV_B = 128
G_C = 4
DG_C = 128
ROPE_BASE = 10000.0
SEG = 512
(S_QA, S_FFA, S_FBA, S_IA, S_ZA, S_QB, S_KB, S_VB, S_ZB, S_UC, S_ZC) = range(11)
S_GATE = 11
PROJ_SEGS = (S_QA, S_FFA, S_FBA, S_IA, S_QB, S_KB, S_VB, S_UC)
MERGE_SEGS = (S_ZA, S_ZB, S_ZC) + tuple(range(S_GATE, S_GATE + 6))

CHUNK = 128
SUB = 16
HEADS_PER_STEP = 4
TOKEN_TILE = 512
VMEM_LIMIT = 56 * 1024 * 1024


def _cparams(sem):
    return pltpu.CompilerParams(dimension_semantics=sem, vmem_limit_bytes=VMEM_LIMIT)


def _resident(shape):
    return pl.BlockSpec(shape, lambda *_: (0,) * len(shape), pipeline_mode=pl.Buffered(1))


def _silu(x):
    return x * jax.nn.sigmoid(x)


def _log_sigmoid(x):
    return jnp.minimum(x, 0.0) - jnp.log1p(jnp.exp(-jnp.abs(x)))


def _dot(a, b):
    return jnp.dot(a, b, preferred_element_type=F32)


def _dot_nt(a, b):
    return lax.dot_general(a, b, (((1,), (1,)), ((), ())), preferred_element_type=F32)


def _modulated_norm(x, g, mod_ref):
    y = x * lax.rsqrt(jnp.mean(x * x, axis=-1, keepdims=True) + EPS) * g
    shift = mod_ref[0, 0:1, :]
    scale = mod_ref[0, 1:2, :]
    return (y * (1.0 + scale) + shift).astype(BF16)


def _mod_kernel(c_ref, w_ref, b_ref, o_ref):
    c = c_ref[...]
    s = _silu(c).astype(BF16)
    o_ref[0] = _dot(s, w_ref[0].astype(BF16)) + b_ref[0]


def _modulation(cond8, ada_w, ada_b):
    tn = 512
    return pl.pallas_call(
        _mod_kernel,
        grid=(DEPTH, 3 * D_MODEL // tn),
        in_specs=[
            pl.BlockSpec((8, D_MODEL), lambda l, j: (0, 0)),
            pl.BlockSpec((1, D_MODEL, tn), lambda l, j: (l, 0, j)),
            pl.BlockSpec((1, 1, tn), lambda l, j: (l, 0, j)),
        ],
        out_specs=pl.BlockSpec((1, 8, tn), lambda l, j: (l, 0, j)),
        out_shape=jax.ShapeDtypeStruct((DEPTH, 8, 3 * D_MODEL), F32),
        compiler_params=_cparams(("arbitrary", "arbitrary")),
        name="adaln_mod",
    )(cond8, ada_w, ada_b.reshape(DEPTH, 1, 3 * D_MODEL))


def _proj_kernel(x_ref, mod_ref, g_ref, la_ref, l1_ref, *refs):
    n_seg = len(PROJ_SEGS)
    w_refs, o_refs = refs[:n_seg], refs[n_seg:]
    h = _modulated_norm(x_ref[...], g_ref[...], mod_ref)

    def log_f(raw, d):
        a = la_ref[d:d + 1, :]
        b = l1_ref[d:d + 1, :] + _log_sigmoid(raw)
        return jnp.maximum(a, b) + jnp.log1p(jnp.exp(-jnp.abs(a - b)))

    for s, w_ref, o_ref in zip(PROJ_SEGS, w_refs, o_refs):
        acc = _dot(h, w_ref[0])
        if s == S_QA:
            acc = _silu(acc)
        elif s in (S_FFA, S_FBA):
            acc = log_f(acc, s - S_FFA)
        o_ref[...] = acc


def _segment_specs(layer, segs):
    return [pl.BlockSpec((1, D_MODEL, SEG), (lambda *_, s=s: (layer, 0, s)),
                         pipeline_mode=pl.Buffered(1)) for s in segs]


def _in_proj(x2, mod3, row_of_tile, norm_g, w_in_bf, layer, log_lb, log1m_lb):
    n = x2.shape[0]
    tm = TOKEN_TILE
    tok = lambda i: (i, 0)
    n_seg = len(PROJ_SEGS)
    return pl.pallas_call(
        _proj_kernel,
        grid=(n // tm,),
        in_specs=[
            pl.BlockSpec((tm, D_MODEL), tok),
            pl.BlockSpec((1, 3, D_MODEL), lambda i: (row_of_tile(i), 0, 0)),
            _resident((1, D_MODEL)),
            _resident((2, SEG)),
            _resident((2, SEG)),
        ] + _segment_specs(layer, PROJ_SEGS),
        out_specs=[pl.BlockSpec((tm, SEG), tok)] * n_seg,
        out_shape=[jax.ShapeDtypeStruct((n, SEG), F32)] * n_seg,
        compiler_params=_cparams(("arbitrary",)),
        name="in_proj",
    )(x2, mod3, norm_g.reshape(1, D_MODEL), log_lb, log1m_lb, *([w_in_bf] * n_seg))


def _split3(x):
    hi = x.astype(BF16)
    r = x - hi.astype(F32)
    mid = r.astype(BF16)
    lo = (r - mid.astype(F32)).astype(BF16)
    return hi, mid, lo


def _level_factors(q, k, g, m, reverse, diag):
    nb = CHUNK // m
    fq, fk = [], []
    zeros = jnp.zeros((m, 128), BF16)
    for i in range(nb):
        rows = slice(i * m, (i + 1) * m)
        sl = g[rows]

        def scaled(x, e):
            return (x[rows] * jnp.exp(e)).astype(BF16)

        if diag:
            r = i * m + (m // 2 - 1 if not reverse else m // 2)
            ref = g[r:r + 1]
            fq.append(scaled(q, sl - ref))
            fk.append(scaled(k, ref - sl))
        elif (i % 2 == 1) != reverse:
            r = (i + 1) * m if reverse else i * m - 1
            fq.append(scaled(q, sl - g[r:r + 1]))
            fk.append(zeros)
        else:
            r = i * m if reverse else (i + 1) * m - 1
            fq.append(zeros)
            fk.append(scaled(k, g[r:r + 1] - sl))
    return jnp.concatenate(fq, axis=0), jnp.concatenate(fk, axis=0)


def _level_masks(reverse):
    t = lax.broadcasted_iota(jnp.int32, (CHUNK, CHUNK), 0)
    s = lax.broadcasted_iota(jnp.int32, (CHUNK, CHUNK), 1)
    sh = int(math.log2(SUB))
    same = (t >> sh) == (s >> sh)
    masks = [same & ((s >= t) if reverse else (s <= t))]
    m = SUB
    while m < CHUNK:
        sh = int(math.log2(m))
        tb, sb = t >> sh, s >> sh
        if not reverse:
            masks.append(((tb & 1) == 1) & (sb == tb - 1))
        else:
            masks.append(((tb & 1) == 0) & (sb == tb + 1))
        m *= 2
    return masks


def _hgrn_chunk(lf, q, v, st, tri, masks, reverse):
    end_row = 0 if reverse else CHUNK - 1
    k = 1.0 - jnp.exp(lf)
    g = _dot(tri, jnp.concatenate(_split3(lf), axis=0))
    g_end = g[end_row:end_row + 1]
    vt_bf = v.T.astype(BF16)

    a = jnp.zeros((CHUNK, CHUNK), F32)
    m = SUB
    for lvl, mask in enumerate(masks):
        fq, fk = _level_factors(q, k, g, m, reverse, diag=(lvl == 0))
        a = jnp.where(mask, _dot_nt(fq, fk), a)
        if lvl > 0:
            m *= 2

    qg = (q * jnp.exp(g)).astype(BF16)
    o = _dot_nt(jnp.concatenate([qg, a.astype(BF16)], axis=1),
                jnp.concatenate([st.astype(BF16), vt_bf], axis=1))

    kg = (k * jnp.exp(g_end - g)).astype(BF16)
    u_t = _dot(vt_bf, kg)
    return o, jnp.exp(g_end) * st + u_t


def _hgrn_kernel(*refs, n_chunks, has_state, want_state, n_prev):
    it = iter(refs)
    q_ref, lff_ref, lfb_ref, v_ref, g_ref = (next(it) for _ in range(5))
    s0_ref = next(it) if has_state else None
    prev_refs = [next(it) for _ in range(n_prev)]
    y_ref = next(it)
    sfin_ref = next(it) if want_state else None
    o_scr, st_scr = next(it), next(it)
    hpb = HEADS_PER_STEP

    t_i = lax.broadcasted_iota(jnp.int32, (CHUNK, CHUNK), 0)
    s_i = lax.broadcasted_iota(jnp.int32, (CHUNK, CHUNK), 1)
    tris = [jnp.where(s_i <= t_i, 1.0, 0.0).astype(BF16), jnp.where(s_i >= t_i, 1.0, 0.0).astype(BF16)]
    tris = [jnp.concatenate([tr, tr, tr], axis=1) for tr in tris]
    masks = [_level_masks(False), _level_masks(True)]
    lf_refs = (lff_ref, lfb_ref)

    for hh in range(hpb):
        for d in range(2):
            if has_state:
                st_scr[2 * hh + d] = s0_ref[0, d, hh].T
            else:
                st_scr[2 * hh + d] = jnp.zeros((DV_A, DK_A), F32)

    def step(ci):
        for hh in range(hpb):
            ls = slice(hh * 128, (hh + 1) * 128)
            for d in range(2):
                c = ci if d == 0 else n_chunks - 1 - ci
                r0 = c * CHUNK
                if not isinstance(r0, int):
                    r0 = pl.multiple_of(r0, CHUNK)
                rows = pl.ds(r0, CHUNK)
                o, st = _hgrn_chunk(lf_refs[d][0, rows, ls], q_ref[0, rows, ls], v_ref[0, rows, ls],
                                    st_scr[2 * hh + d], tris[d], masks[d], reverse=(d == 1))
                o_scr[d, rows, ls] = o
                st_scr[2 * hh + d] = st

    if n_chunks <= 2:
        for ci in range(n_chunks):
            step(ci)
    else:
        def body(i, carry):
            step(2 * i)
            step(2 * i + 1)
            return carry
        lax.fori_loop(0, n_chunks // 2, body, 0)

    for j, p_ref in enumerate(prev_refs):
        sfin_ref[0, j] = p_ref[0]
    for hh in range(hpb):
        ls = slice(hh * 128, (hh + 1) * 128)
        if want_state:
            for d in range(2):
                if n_prev:
                    sfin_ref[0, n_prev, d, hh] = st_scr[2 * hh + d].T
                else:
                    sfin_ref[0, d, hh] = st_scr[2 * hh + d].T
        o = o_scr[0, :, ls] + o_scr[1, :, ls]
        y_ref[0, :, ls] = o * lax.rsqrt(jnp.mean(o * o, axis=-1, keepdims=True) + EPS) * g_ref[0, :, ls]


def _hgrn(q3, lff3, lfb3, i3, hgrn_g, s0, want_state, prev_states=()):
    b, t, _ = q3.shape
    n_chunks = t // CHUNK
    has_state = s0 is not None
    hpb = HEADS_PER_STEP
    w = 128 * hpb
    n_prev = len(prev_states)
    col = pl.BlockSpec((1, t, w), lambda bi, hp: (bi, 0, hp))

    in_specs = [col, col, col, col, pl.BlockSpec((1, 1, w), lambda bi, hp: (hp, 0, 0))]
    args = [q3, lff3, lfb3, i3, hgrn_g.reshape(H_A // hpb, 1, w)]
    st_spec = pl.BlockSpec((1, 2, hpb, DK_A, DV_A), lambda bi, hp: (bi, 0, hp, 0, 0))
    if has_state:
        in_specs.append(st_spec)
        args.append(s0)
    in_specs += [st_spec] * n_prev
    args += list(prev_states)
    out_specs = [pl.BlockSpec((1, t, w), lambda bi, hp: (bi, 0, hp))]
    out_shape = [jax.ShapeDtypeStruct((b, t, H_A * DV_A), F32)]
    if want_state and n_prev:
        out_specs.append(pl.BlockSpec((1, n_prev + 1, 2, hpb, DK_A, DV_A),
                                      lambda bi, hp: (bi, 0, 0, hp, 0, 0)))
        out_shape.append(jax.ShapeDtypeStruct((b, n_prev + 1, 2, H_A, DK_A, DV_A), F32))
    elif want_state:
        out_specs.append(st_spec)
        out_shape.append(jax.ShapeDtypeStruct((b, 2, H_A, DK_A, DV_A), F32))
    res = pl.pallas_call(
        functools.partial(_hgrn_kernel, n_chunks=n_chunks, has_state=has_state,
                          want_state=want_state, n_prev=n_prev),
        grid=(b, H_A // hpb),
        in_specs=in_specs,
        out_specs=out_specs,
        out_shape=out_shape,
        scratch_shapes=[pltpu.VMEM((2, t, w), F32), pltpu.VMEM((2 * hpb, DV_A, DK_A), F32)],
        compiler_params=_cparams(("arbitrary", "arbitrary")),
        name="hgrn2",
    )(*args)
    return res if want_state else (res[0], None)


def _half_sum_matrix():
    r = lax.broadcasted_iota(jnp.int32, (128, 128), 0)
    c = lax.broadcasted_iota(jnp.int32, (128, 128), 1)
    return jnp.where((r >> 6) == (c >> 6), 1.0, 0.0).astype(BF16)


def _rms64(x, g, half_sum):
    x2 = x * x
    hi = x2.astype(BF16)
    lo = (x2 - hi.astype(F32)).astype(BF16)
    ss = _dot(hi, half_sum) + _dot(lo, half_sum)
    return x * lax.rsqrt(ss / DH_B + EPS) * g


def _rope(x, cos, sin):
    lane = lax.broadcasted_iota(jnp.int32, x.shape, 1)
    first = (lane & 31) < 16
    partner = jnp.where(first, pltpu.roll(x, 112, 1), pltpu.roll(x, 16, 1))
    return x * cos + partner * sin


def _attn_kernel(*refs, t_self, t_ctx, lam_init, latent, kv_out, n_prev):
    it = iter(refs)
    q_ref, k_ref, v_ref = (next(it) for _ in range(3))
    qg_ref, kg_ref, sg_ref, lam_ref = (next(it) for _ in range(4))
    if latent:
        cq_ref, sq_ref, ck_ref, sk_ref, kc_ref, vc_ref = (next(it) for _ in range(6))
    prev_k = [next(it) for _ in range(n_prev)]
    prev_v = [next(it) for _ in range(n_prev)]
    y_ref = next(it)
    kout_ref = next(it) if kv_out else None
    vout_ref = next(it) if kv_out == "stack" else None
    k_scr, v_scr = next(it), next(it)

    qi = pl.program_id(1)
    half_sum = _half_sum_matrix()
    t_all = t_self + t_ctx

    @pl.when(qi == 0)
    def _():
        for j in range(n_prev):
            kout_ref[0, j] = prev_k[j][0]
            vout_ref[0, j] = prev_v[j][0]
        if kv_out == "stack":
            vout_ref[0, n_prev] = v_ref[0]
        for h in range(H_B):
            ls = slice(h * 128, (h + 1) * 128)
            vs = slice(h * 256, h * 256 + 128)
            kn = _rms64(k_ref[0, :, ls], kg_ref[...], half_sum)
            if kv_out == "stack":
                kout_ref[0, n_prev, :, ls] = kn
            elif kv_out:
                kout_ref[0, :, ls] = kn
            if latent:
                kn = _rope(kn, ck_ref[...], sk_ref[...])
                k_scr[t_self:t_all, ls] = kc_ref[0, 0, :, ls].astype(BF16)
                v_scr[t_self:t_all, vs] = vc_ref[0, 0, :, ls].astype(BF16)
            k_scr[0:t_self, ls] = kn.astype(BF16)
            v_scr[0:t_self, vs] = v_ref[0, :, ls].astype(BF16)
            v_scr[:, h * 256 + 128:(h + 1) * 256] = jnp.ones((t_all, 128), BF16)

    lp = lam_ref[...]
    l01 = jnp.sum(lp[0:1] * lp[1:2], axis=-1, keepdims=True)
    l23 = jnp.sum(lp[2:3] * lp[3:4], axis=-1, keepdims=True)
    lam = jnp.exp(l01) - jnp.exp(l23) + lam_init

    for h in range(H_B):
        ls = slice(h * 128, (h + 1) * 128)
        qn = _rms64(q_ref[0, :, ls], qg_ref[...], half_sum)
        if latent:
            qn = _rope(qn, cq_ref[...], sq_ref[...])
        qn = qn * (DH_B ** -0.5)
        lane = lax.broadcasted_iota(jnp.int32, qn.shape, 1)
        kf = k_scr[:, ls]
        v1 = v_scr[:, h * 256:(h + 1) * 256]

        def unnormalised(qm):
            s = _dot_nt(qm.astype(BF16), kf)
            e = jnp.exp(s - jnp.max(s, axis=-1, keepdims=True)).astype(BF16)
            oz = _dot(e, v1)
            return oz[:, 0:128], oz[:, 128:256]

        o0, z0 = unnormalised(jnp.where(lane < DH_B, qn, 0.0))
        o1, z1 = unnormalised(jnp.where(lane < DH_B, 0.0, qn))
        o = o0 * (1.0 / z0) - o1 * (lam / z1)
        y = o * lax.rsqrt(jnp.mean(o * o, axis=-1, keepdims=True) + EPS) * sg_ref[...]
        y_ref[0, :, ls] = y * (1.0 - lam_init)


def _diff_attn(q3, k3, v3, layer, qn_g, kn_g, sub_g, lam_p, rope_tab, cache_k, cache_v, kv_out,
               prev_k=(), prev_v=()):
    b, t, _ = q3.shape
    latent = rope_tab is not None
    tq = 256
    t_ctx = cache_k.shape[2] if latent else 0
    lam_init = 0.8 - 0.6 * math.exp(-0.3 * layer)
    n_prev = len(prev_k)
    w = H_B * 128

    def small(shape):
        return pl.BlockSpec(shape, lambda bi, qi: (0,) * len(shape))

    seq_blk = pl.BlockSpec((1, t, w), lambda bi, qi: (bi, 0, 0))
    in_specs = [
        pl.BlockSpec((1, tq, w), lambda bi, qi: (bi, qi, 0)),
        seq_blk, seq_blk,
        small((1, 128)), small((1, 128)), small((1, 128)), small((4, DH_B)),
    ]
    args = [q3, k3, v3, jnp.tile(qn_g, 2).reshape(1, 128), jnp.tile(kn_g, 2).reshape(1, 128),
            sub_g.reshape(1, 128), lam_p]
    if latent:
        cos_t, sin_t = rope_tab
        in_specs += [
            pl.BlockSpec((tq, 128), lambda bi, qi: (qi, 0)),
            pl.BlockSpec((tq, 128), lambda bi, qi: (qi, 0)),
            small((t, 128)), small((t, 128)),
            pl.BlockSpec((1, 1, t_ctx, w), lambda bi, qi: (bi, layer, 0, 0)),
            pl.BlockSpec((1, 1, t_ctx, w), lambda bi, qi: (bi, layer, 0, 0)),
        ]
        args += [cos_t, sin_t, cos_t, sin_t, cache_k, cache_v]
    in_specs += [seq_blk] * (2 * n_prev)
    args += list(prev_k) + list(prev_v)
    out_specs = [pl.BlockSpec((1, tq, w), lambda bi, qi: (bi, qi, 0))]
    out_shape = [jax.ShapeDtypeStruct((b, t, w), F32)]
    if kv_out == "stack":
        stack_spec = pl.BlockSpec((1, n_prev + 1, t, w), lambda bi, qi: (bi, 0, 0, 0))
        out_specs += [stack_spec, stack_spec]
        out_shape += [jax.ShapeDtypeStruct((b, n_prev + 1, t, w), F32)] * 2
    elif kv_out:
        out_specs.append(seq_blk)
        out_shape.append(jax.ShapeDtypeStruct((b, t, w), F32))
    return pl.pallas_call(
        functools.partial(_attn_kernel, t_self=t, t_ctx=t_ctx, lam_init=lam_init,
                          latent=latent, kv_out=kv_out, n_prev=n_prev),
        grid=(b, t // tq),
        in_specs=in_specs,
        out_specs=out_specs,
        out_shape=out_shape,
        scratch_shapes=[pltpu.VMEM((t + t_ctx, w), BF16), pltpu.VMEM((t + t_ctx, 2 * w), BF16)],
        compiler_params=_cparams(("arbitrary", "arbitrary")),
        name="diff_attn",
    )(*args)


def _dft_tables(t):
    def cs(n):
        idx = np.arange(n, dtype=np.int64)
        ang = 2.0 * np.pi * ((idx[:, None] * idx[None, :]) % n).astype(np.float64) / n
        return np.cos(ang), np.sin(ang)

    cc, sc = cs(DG_C)
    ct, st = cs(t)
    chan = jnp.asarray(np.concatenate([cc, sc], axis=1), dtype=F32)
    pos = jnp.asarray(np.concatenate([ct, -st], axis=1), dtype=F32)
    return chan, pos


def _fourier_kernel(u_ref, chan_ref, pos_ref, y_ref, uc_scr, *, t, scale):
    ri = pl.program_id(1)

    @pl.when(ri == 0)
    def _():
        for g in range(G_C):
            ls = slice(g * DG_C, (g + 1) * DG_C)
            u1 = _dot(u_ref[0, :, ls].astype(BF16), chan_ref[...].astype(BF16))
            uc_scr[0:t, ls] = u1[:, 0:DG_C].astype(BF16)
            uc_scr[t:2 * t, ls] = u1[:, DG_C:2 * DG_C].astype(BF16)

    y_ref[0] = _dot(pos_ref[...].astype(BF16), uc_scr[...]) * scale


def _fourier(pc3):
    b, t, w = pc3.shape
    tr = min(t, 512)
    chan, pos = _dft_tables(t)
    scale = 1.0 / math.sqrt(t * DG_C)
    return pl.pallas_call(
        functools.partial(_fourier_kernel, t=t, scale=scale),
        grid=(b, t // tr),
        in_specs=[
            pl.BlockSpec((1, t, w), lambda bi, ri: (bi, 0, 0)),
            _resident((DG_C, 2 * DG_C)),
            pl.BlockSpec((tr, 2 * t), lambda bi, ri: (ri, 0)),
        ],
        out_specs=pl.BlockSpec((1, tr, w), lambda bi, ri: (bi, ri, 0)),
        out_shape=jax.ShapeDtypeStruct((b, t, w), F32),
        scratch_shapes=[pltpu.VMEM((2 * t, w), BF16)],
        compiler_params=_cparams(("arbitrary", "arbitrary")),
        name="fourier_mix",
    )(pc3, chan, pos)


def _merge_kernel(x_ref, mod_ref, g_ref, oa_ref, ob_ref, oc_ref, *refs):
    n_seg = len(MERGE_SEGS)
    wm_refs = refs[:n_seg]
    wa_ref, wb_ref, wc_ref, wo_ref, o_ref = refs[n_seg:]
    x = x_ref[...]
    h = _modulated_norm(x, g_ref[...], mod_ref)

    def seg(s):
        return _dot(h, wm_refs[s][0])

    ya = (oa_ref[...] * _silu(seg(0))).astype(BF16)
    yb = (ob_ref[...] * _silu(seg(1))).astype(BF16)
    yc = (oc_ref[...] * _silu(seg(2))).astype(BF16)
    parts = []
    for n in range(2):
        cols = slice(n * SEG, (n + 1) * SEG)
        m = jax.nn.sigmoid(seg(3 + n)) * _dot(ya, wa_ref[:, cols])
        m = m + jax.nn.sigmoid(seg(5 + n)) * _dot(yb, wb_ref[:, cols])
        m = m + jax.nn.sigmoid(seg(7 + n)) * _dot(yc, wc_ref[:, cols])
        parts.append(m.astype(BF16))
    merged = jnp.concatenate(parts, axis=1)
    gate = mod_ref[0, 2:3, :]
    o_ref[...] = x + gate * _dot(merged, wo_ref[...])


def _merge(x2, mod3, row_of_tile, norm_g, oa, ob, oc, w_in_bf, layer, wa, wb, wc, wo):
    n = x2.shape[0]
    tm = TOKEN_TILE
    tok = lambda i: (i, 0)
    in_specs = [
        pl.BlockSpec((tm, D_MODEL), tok),
        pl.BlockSpec((1, 3, D_MODEL), lambda i: (row_of_tile(i), 0, 0)),
        _resident((1, D_MODEL)),
        pl.BlockSpec((tm, SEG), tok), pl.BlockSpec((tm, SEG), tok), pl.BlockSpec((tm, SEG), tok),
    ] + _segment_specs(layer, MERGE_SEGS) + [
        _resident((SEG, D_MODEL)), _resident((SEG, D_MODEL)), _resident((SEG, D_MODEL)),
        _resident((D_MODEL, D_MODEL)),
    ]
    return pl.pallas_call(
        _merge_kernel,
        grid=(n // tm,),
        in_specs=in_specs,
        out_specs=pl.BlockSpec((tm, D_MODEL), tok),
        out_shape=jax.ShapeDtypeStruct((n, D_MODEL), F32),
        compiler_params=_cparams(("arbitrary",)),
        name="merge_out",
    )(x2, mod3, norm_g.reshape(1, D_MODEL), oa, ob, oc, *([w_in_bf] * len(MERGE_SEGS)),
      wa, wb, wc, wo)


def _rope_tables(t):
    nf = DH_B // 4
    n_rows = t // GRID_W
    pos_row = jnp.repeat(jnp.arange(n_rows, dtype=F32), GRID_W)
    pos_col = jnp.tile(jnp.arange(GRID_W, dtype=F32), n_rows)
    inv = ROPE_BASE ** (-jnp.arange(nf, dtype=F32) / nf)
    ar = pos_row[:, None] * inv[None, :]
    ac = pos_col[:, None] * inv[None, :]
    cos64 = jnp.concatenate([jnp.cos(ar), jnp.cos(ar), jnp.cos(ac), jnp.cos(ac)], axis=1)
    sin64 = jnp.concatenate([-jnp.sin(ar), jnp.sin(ar), -jnp.sin(ac), jnp.sin(ac)], axis=1)
    return jnp.tile(cos64, (1, 2)), jnp.tile(sin64, (1, 2))


def _layer(x, mod3, row_of_tile, layer, lw, log_lb, log1m_lb, ctx=None, rope_tab=None, prev=None):
    (norm_g, w_in_bf, hgrn_g, qn_g, kn_g, lam_p, sub_g, wa, wb, wc, wo) = lw
    b, t, _ = x.shape
    x2 = x.reshape(b * t, D_MODEL)
    segs = _in_proj(x2, mod3, row_of_tile, norm_g, w_in_bf, layer, log_lb, log1m_lb)
    qa, lff, lfb, ia, qb, kb, vb, uc = (s.reshape(b, t, SEG) for s in segs)
    attn = functools.partial(_diff_attn, qb, kb, vb, layer, qn_g, kn_g, sub_g, lam_p)
    is_ctx = ctx is None
    if not is_ctx:
        oa, _ = _hgrn(qa, lff, lfb, ia, hgrn_g, ctx[2], want_state=False)
        (ob,) = attn(rope_tab, ctx[0], ctx[1], None)
    elif prev is None:
        oa, s_out = _hgrn(qa, lff, lfb, ia, hgrn_g, None, want_state=True)
        ob, k_out = attn(None, None, None, "plain")
        v_out = vb
    else:
        oa, s_out = _hgrn(qa, lff, lfb, ia, hgrn_g, None, want_state=True, prev_states=prev[2])
        ob, k_out, v_out = attn(None, None, None, "stack", prev_k=prev[0], prev_v=prev[1])
    oc = _fourier(uc)
    x_new = _merge(x2, mod3, row_of_tile, norm_g, oa.reshape(b * t, SEG), ob.reshape(b * t, SEG),
                   oc.reshape(b * t, SEG), w_in_bf, layer, wa, wb, wc, wo)
    x_new = x_new.reshape(b, t, D_MODEL)
    if is_ctx:
        return x_new, k_out, v_out, s_out
    return x_new


def kernel(x_prompt, x_sample, c, cache_diff_k, cache_diff_v, state_hgrn, c_ctx, ada_w, ada_b,
           norm_g, w_in, hgrn_lb, hgrn_norm_g, diff_qn_g, diff_kn_g, diff_lambda, diff_subln_g,
           w_branch_a, w_branch_b, w_branch_c, w_out):
    batch, seq, _ = x_prompt.shape
    dec_b, dec_t, _ = x_sample.shape
    past = cache_diff_k.shape[2]

    lb_all = jnp.cumsum(jax.nn.softmax(hgrn_lb.astype(F32), axis=0), axis=0)
    lb_all = lb_all - lb_all[0]
    log_lb = jnp.log(lb_all)
    log1m_lb = jnp.log1p(-lb_all)

    cond8 = jnp.zeros((8, D_MODEL), F32).at[0].set(c_ctx).at[1:1 + dec_b].set(c)
    mod = _modulation(cond8, ada_w, ada_b).reshape(DEPTH, 8, 3, D_MODEL)

    rope_tab = _rope_tables(dec_t)
    cache_k4 = cache_diff_k.reshape(dec_b, DEPTH, past, H_B * 2 * DH_B)
    cache_v4 = cache_diff_v.reshape(dec_b, DEPTH, past, H_B * DV_B)
    tiles_per_seq = dec_t // TOKEN_TILE
    ctx_row = lambda i: 0
    lat_row = lambda i: 1 + i // tiles_per_seq

    xp, xs = x_prompt, x_sample
    prev_k, prev_pb, prev_s = [], [], []
    w_in_bf = w_in.astype(BF16)
    for l in range(DEPTH):
        lw = (norm_g[l], w_in_bf,
              hgrn_norm_g[l], diff_qn_g[l], diff_kn_g[l], diff_lambda[l], diff_subln_g[l],
              w_branch_a[l].astype(BF16), w_branch_b[l].astype(BF16),
              w_branch_c[l].astype(BF16), w_out[l].astype(BF16))
        mod3 = mod[l]
        last = l == DEPTH - 1
        xp, k_c, v_c, s_c = _layer(xp, mod3, ctx_row, l, lw, log_lb[l], log1m_lb[l],
                                   prev=(prev_k, prev_pb, prev_s) if last and l > 0 else None)
        if not last:
            prev_k.append(k_c)
            prev_pb.append(v_c)
            prev_s.append(s_c)
        xs = _layer(xs, mod3, lat_row, l, lw, log_lb[l], log1m_lb[l],
                    ctx=(cache_k4, cache_v4, state_hgrn[:, l]), rope_tab=rope_tab)
    new_k = k_c.reshape(batch, DEPTH, seq, H_B, 2, DH_B)
    new_v = v_c.reshape(batch, DEPTH, seq, H_B, DV_B)
    return (xp, xs, new_k, new_v, s_c)
```

```python
import functools
import math

import numpy as np
import jax
import jax.numpy as jnp
from jax import lax
from jax.experimental import pallas as pl
from jax.experimental.pallas import tpu as pltpu

F32 = jnp.float32
BF16 = jnp.bfloat16

D_MODEL = 1024
DEPTH = 2
GRID_W = 64
EPS = 1e-6
H_A = 4
DK_A = 128
DV_A = 128
H_B = 4
DH_B = 64
DV_B = 128
G_C = 4
DG_C = 128
ROPE_BASE = 10000.0
SEG = 512
(S_QA, S_FFA, S_FBA, S_IA, S_ZA, S_QB, S_KB, S_VB, S_ZB, S_UC, S_ZC) = range(11)
S_GATE = 11
PROJ_SEGS = (S_QA, S_FFA, S_FBA, S_IA, S_QB, S_KB, S_VB, S_UC)
MERGE_SEGS = (S_ZA, S_ZB, S_ZC) + tuple(range(S_GATE, S_GATE + 6))

CHUNK = 128
SUB = 16
HEADS_PER_STEP = 4
TOKEN_TILE = 512
ATTN_SEQS_PER_STEP = 4
FOURIER_SEQS_PER_STEP = 4
VMEM_LIMIT = 56 * 1024 * 1024


def _cparams(sem):
    return pltpu.CompilerParams(dimension_semantics=sem, vmem_limit_bytes=VMEM_LIMIT)


def _resident(shape):
    return pl.BlockSpec(shape, lambda *_: (0,) * len(shape), pipeline_mode=pl.Buffered(1))


def _silu(x):
    return x * jax.nn.sigmoid(x)


def _log_sigmoid(x):
    return jnp.minimum(x, 0.0) - jnp.log1p(jnp.exp(-jnp.abs(x)))


def _dot(a, b):
    return jnp.dot(a, b, preferred_element_type=F32)


def _dot_nt(a, b):
    return lax.dot_general(a, b, (((1,), (1,)), ((), ())), preferred_element_type=F32)


def _modulated_norm(x, g, mod_ref):
    y = x * lax.rsqrt(jnp.mean(x * x, axis=-1, keepdims=True) + EPS) * g
    shift = mod_ref[0, 0:1, :]
    scale = mod_ref[0, 1:2, :]
    return (y * (1.0 + scale) + shift).astype(BF16)


def _mod_kernel(c_ref, w_ref, b_ref, o_ref):
    c = c_ref[...]
    s = _silu(c).astype(BF16)
    o_ref[0] = _dot(s, w_ref[0].astype(BF16)) + b_ref[0]


def _modulation(cond8, ada_w, ada_b):
    tn = 512
    return pl.pallas_call(
        _mod_kernel,
        grid=(DEPTH, 3 * D_MODEL // tn),
        in_specs=[
            pl.BlockSpec((8, D_MODEL), lambda l, j: (0, 0)),
            pl.BlockSpec((1, D_MODEL, tn), lambda l, j: (l, 0, j)),
            pl.BlockSpec((1, 1, tn), lambda l, j: (l, 0, j)),
        ],
        out_specs=pl.BlockSpec((1, 8, tn), lambda l, j: (l, 0, j)),
        out_shape=jax.ShapeDtypeStruct((DEPTH, 8, 3 * D_MODEL), F32),
        compiler_params=_cparams(("arbitrary", "arbitrary")),
        name="adaln_mod",
    )(cond8, ada_w, ada_b.reshape(DEPTH, 1, 3 * D_MODEL))


def _proj_kernel(x_ref, mod_ref, g_ref, la_ref, l1_ref, *refs):
    n_seg = len(PROJ_SEGS)
    w_refs, o_refs = refs[:n_seg], refs[n_seg:]
    h = _modulated_norm(x_ref[...], g_ref[...], mod_ref)

    def log_f(raw, d):
        a = la_ref[d:d + 1, :]
        b = l1_ref[d:d + 1, :] + _log_sigmoid(raw)
        return jnp.maximum(a, b) + jnp.log1p(jnp.exp(-jnp.abs(a - b)))

    for s, w_ref, o_ref in zip(PROJ_SEGS, w_refs, o_refs):
        acc = _dot(h, w_ref[0])
        if s == S_QA:
            acc = _silu(acc)
        elif s in (S_FFA, S_FBA):
            acc = log_f(acc, s - S_FFA)
        o_ref[...] = acc


def _segment_specs(layer, segs):
    return [pl.BlockSpec((1, D_MODEL, SEG), (lambda *_, s=s: (layer, 0, s)),
                         pipeline_mode=pl.Buffered(1)) for s in segs]


def _in_proj(x2, mod3, row_of_tile, norm_g, w_in_bf, layer, log_lb, log1m_lb):
    n = x2.shape[0]
    tm = TOKEN_TILE
    tok = lambda i: (i, 0)
    n_seg = len(PROJ_SEGS)
    return pl.pallas_call(
        _proj_kernel,
        grid=(n // tm,),
        in_specs=[
            pl.BlockSpec((tm, D_MODEL), tok),
            pl.BlockSpec((1, 3, D_MODEL), lambda i: (row_of_tile(i), 0, 0)),
            _resident((1, D_MODEL)),
            _resident((2, SEG)),
            _resident((2, SEG)),
        ] + _segment_specs(layer, PROJ_SEGS),
        out_specs=[pl.BlockSpec((tm, SEG), tok)] * n_seg,
        out_shape=[jax.ShapeDtypeStruct((n, SEG), F32)] * n_seg,
        compiler_params=_cparams(("arbitrary",)),
        name="in_proj",
    )(x2, mod3, norm_g.reshape(1, D_MODEL), log_lb, log1m_lb, *([w_in_bf] * n_seg))


def _split3(x):
    hi = x.astype(BF16)
    r = x - hi.astype(F32)
    mid = r.astype(BF16)
    lo = (r - mid.astype(F32)).astype(BF16)
    return hi, mid, lo


def _level_factors(q, k, g, m, reverse, diag):
    nb = CHUNK // m
    fq, fk = [], []
    zeros = jnp.zeros((m, 128), BF16)
    for i in range(nb):
        rows = slice(i * m, (i + 1) * m)
        sl = g[rows]

        def scaled(x, e):
            return (x[rows] * jnp.exp(e)).astype(BF16)

        if diag:
            r = i * m + (m // 2 - 1 if not reverse else m // 2)
            ref = g[r:r + 1]
            fq.append(scaled(q, sl - ref))
            fk.append(scaled(k, ref - sl))
        elif (i % 2 == 1) != reverse:
            r = (i + 1) * m if reverse else i * m - 1
            fq.append(scaled(q, sl - g[r:r + 1]))
            fk.append(zeros)
        else:
            r = i * m if reverse else (i + 1) * m - 1
            fq.append(zeros)
            fk.append(scaled(k, g[r:r + 1] - sl))
    return jnp.concatenate(fq, axis=0), jnp.concatenate(fk, axis=0)


def _level_masks(reverse):
    t = lax.broadcasted_iota(jnp.int32, (CHUNK, CHUNK), 0)
    s = lax.broadcasted_iota(jnp.int32, (CHUNK, CHUNK), 1)
    sh = int(math.log2(SUB))
    same = (t >> sh) == (s >> sh)
    masks = [same & ((s >= t) if reverse else (s <= t))]
    m = SUB
    while m < CHUNK:
        sh = int(math.log2(m))
        tb, sb = t >> sh, s >> sh
        if not reverse:
            masks.append(((tb & 1) == 1) & (sb == tb - 1))
        else:
            masks.append(((tb & 1) == 0) & (sb == tb + 1))
        m *= 2
    return masks


def _hgrn_chunk(lf, q, v, st, tri, masks, reverse):
    end_row = 0 if reverse else CHUNK - 1
    k = 1.0 - jnp.exp(lf)
    g = _dot(tri, jnp.concatenate(_split3(lf), axis=0))
    g_end = g[end_row:end_row + 1]
    vt_bf = v.T.astype(BF16)

    a = jnp.zeros((CHUNK, CHUNK), F32)
    m = SUB
    for lvl, mask in enumerate(masks):
        fq, fk = _level_factors(q, k, g, m, reverse, diag=(lvl == 0))
        a = jnp.where(mask, _dot_nt(fq, fk), a)
        if lvl > 0:
            m *= 2

    qg = (q * jnp.exp(g)).astype(BF16)
    o = _dot_nt(jnp.concatenate([qg, a.astype(BF16)], axis=1),
                jnp.concatenate([st.astype(BF16), vt_bf], axis=1))

    kg = (k * jnp.exp(g_end - g)).astype(BF16)
    u_t = _dot(vt_bf, kg)
    return o, jnp.exp(g_end) * st + u_t


def _hgrn_kernel(*refs, n_chunks, has_state, want_state, n_prev):
    it = iter(refs)
    q_ref, lff_ref, lfb_ref, v_ref, g_ref = (next(it) for _ in range(5))
    s0_ref = next(it) if has_state else None
    prev_refs = [next(it) for _ in range(n_prev)]
    y_ref = next(it)
    sfin_ref = next(it) if want_state else None
    o_scr, st_scr = next(it), next(it)
    hpb = HEADS_PER_STEP

    t_i = lax.broadcasted_iota(jnp.int32, (CHUNK, CHUNK), 0)
    s_i = lax.broadcasted_iota(jnp.int32, (CHUNK, CHUNK), 1)
    tris = [jnp.where(s_i <= t_i, 1.0, 0.0).astype(BF16), jnp.where(s_i >= t_i, 1.0, 0.0).astype(BF16)]
    tris = [jnp.concatenate([tr, tr, tr], axis=1) for tr in tris]
    masks = [_level_masks(False), _level_masks(True)]
    lf_refs = (lff_ref, lfb_ref)

    for hh in range(hpb):
        for d in range(2):
            if has_state:
                st_scr[2 * hh + d] = s0_ref[0, d, hh].T
            else:
                st_scr[2 * hh + d] = jnp.zeros((DV_A, DK_A), F32)

    def step(ci):
        for hh in range(hpb):
            ls = slice(hh * 128, (hh + 1) * 128)
            for d in range(2):
                c = ci if d == 0 else n_chunks - 1 - ci
                r0 = c * CHUNK
                if not isinstance(r0, int):
                    r0 = pl.multiple_of(r0, CHUNK)
                rows = pl.ds(r0, CHUNK)
                o, st = _hgrn_chunk(lf_refs[d][0, rows, ls], q_ref[0, rows, ls], v_ref[0, rows, ls],
                                    st_scr[2 * hh + d], tris[d], masks[d], reverse=(d == 1))
                o_scr[d, rows, ls] = o
                st_scr[2 * hh + d] = st

    if n_chunks <= 2:
        for ci in range(n_chunks):
            step(ci)
    else:
        def body(i, carry):
            step(2 * i)
            step(2 * i + 1)
            return carry
        lax.fori_loop(0, n_chunks // 2, body, 0)

    for j, p_ref in enumerate(prev_refs):
        sfin_ref[0, j] = p_ref[0]
    for hh in range(hpb):
        ls = slice(hh * 128, (hh + 1) * 128)
        if want_state:
            for d in range(2):
                if n_prev:
                    sfin_ref[0, n_prev, d, hh] = st_scr[2 * hh + d].T
                else:
                    sfin_ref[0, d, hh] = st_scr[2 * hh + d].T
        o = o_scr[0, :, ls] + o_scr[1, :, ls]
        y_ref[0, :, ls] = o * lax.rsqrt(jnp.mean(o * o, axis=-1, keepdims=True) + EPS) * g_ref[0, :, ls]


def _hgrn(q3, lff3, lfb3, i3, hgrn_g, s0, want_state, prev_states=()):
    b, t, _ = q3.shape
    n_chunks = t // CHUNK
    has_state = s0 is not None
    hpb = HEADS_PER_STEP
    w = 128 * hpb
    n_prev = len(prev_states)
    col = pl.BlockSpec((1, t, w), lambda bi, hp: (bi, 0, hp))

    in_specs = [col, col, col, col, pl.BlockSpec((1, 1, w), lambda bi, hp: (hp, 0, 0))]
    args = [q3, lff3, lfb3, i3, hgrn_g.reshape(H_A // hpb, 1, w)]
    st_spec = pl.BlockSpec((1, 2, hpb, DK_A, DV_A), lambda bi, hp: (bi, 0, hp, 0, 0))
    if has_state:
        in_specs.append(st_spec)
        args.append(s0)
    in_specs += [st_spec] * n_prev
    args += list(prev_states)
    out_specs = [pl.BlockSpec((1, t, w), lambda bi, hp: (bi, 0, hp))]
    out_shape = [jax.ShapeDtypeStruct((b, t, H_A * DV_A), F32)]
    if want_state and n_prev:
        out_specs.append(pl.BlockSpec((1, n_prev + 1, 2, hpb, DK_A, DV_A),
                                      lambda bi, hp: (bi, 0, 0, hp, 0, 0)))
        out_shape.append(jax.ShapeDtypeStruct((b, n_prev + 1, 2, H_A, DK_A, DV_A), F32))
    elif want_state:
        out_specs.append(st_spec)
        out_shape.append(jax.ShapeDtypeStruct((b, 2, H_A, DK_A, DV_A), F32))
    res = pl.pallas_call(
        functools.partial(_hgrn_kernel, n_chunks=n_chunks, has_state=has_state,
                          want_state=want_state, n_prev=n_prev),
        grid=(b, H_A // hpb),
        in_specs=in_specs,
        out_specs=out_specs,
        out_shape=out_shape,
        scratch_shapes=[pltpu.VMEM((2, t, w), F32), pltpu.VMEM((2 * hpb, DV_A, DK_A), F32)],
        compiler_params=_cparams(("arbitrary", "arbitrary")),
        name="hgrn2",
    )(*args)
    return res if want_state else (res[0], None)


def _half_sum_matrix():
    r = lax.broadcasted_iota(jnp.int32, (128, 128), 0)
    c = lax.broadcasted_iota(jnp.int32, (128, 128), 1)
    return jnp.where((r >> 6) == (c >> 6), 1.0, 0.0).astype(BF16)


def _rms64(x, g, half_sum):
    x2 = x * x
    hi = x2.astype(BF16)
    lo = (x2 - hi.astype(F32)).astype(BF16)
    ss = _dot(hi, half_sum) + _dot(lo, half_sum)
    return x * lax.rsqrt(ss / DH_B + EPS) * g


def _rope(x, cos, sin):
    lane = lax.broadcasted_iota(jnp.int32, x.shape, 1)
    first = (lane & 31) < 16
    partner = jnp.where(first, pltpu.roll(x, 112, 1), pltpu.roll(x, 16, 1))
    return x * cos + partner * sin


def _attn_kernel(*refs, t_self, t_ctx, lam_init, latent, kv_out, n_prev):
    it = iter(refs)
    q_ref, k_ref, v_ref = (next(it) for _ in range(3))
    qg_ref, kg_ref, sg_ref, lam_ref = (next(it) for _ in range(4))
    if latent:
        cq_ref, sq_ref, ck_ref, sk_ref, kc_ref, vc_ref = (next(it) for _ in range(6))
    prev_k = [next(it) for _ in range(n_prev)]
    prev_v = [next(it) for _ in range(n_prev)]
    y_ref = next(it)
    kout_ref = next(it) if kv_out else None
    vout_ref = next(it) if kv_out == "stack" else None
    k_scr, v_scr = next(it), next(it)

    qi = pl.program_id(1)
    half_sum = _half_sum_matrix()
    t_all = t_self + t_ctx
    spb = q_ref.shape[0]

    @pl.when(qi == 0)
    def _():
        for sb in range(spb):
            for j in range(n_prev):
                kout_ref[sb, j] = prev_k[j][sb]
                vout_ref[sb, j] = prev_v[j][sb]
            if kv_out == "stack":
                vout_ref[sb, n_prev] = v_ref[sb]
            for h in range(H_B):
                ls = slice(h * 128, (h + 1) * 128)
                vs = slice(h * 256, h * 256 + 128)
                kn = _rms64(k_ref[sb, :, ls], kg_ref[...], half_sum)
                if kv_out == "stack":
                    kout_ref[sb, n_prev, :, ls] = kn
                elif kv_out:
                    kout_ref[sb, :, ls] = kn
                if latent:
                    kn = _rope(kn, ck_ref[...], sk_ref[...])
                    k_scr[sb, t_self:t_all, ls] = kc_ref[sb, 0, :, ls].astype(BF16)
                    v_scr[sb, t_self:t_all, vs] = vc_ref[sb, 0, :, ls].astype(BF16)
                k_scr[sb, 0:t_self, ls] = kn.astype(BF16)
                v_scr[sb, 0:t_self, vs] = v_ref[sb, :, ls].astype(BF16)
                v_scr[sb, :, h * 256 + 128:(h + 1) * 256] = jnp.ones((t_all, 128), BF16)

    lp = lam_ref[...]
    l01 = jnp.sum(lp[0:1] * lp[1:2], axis=-1, keepdims=True)
    l23 = jnp.sum(lp[2:3] * lp[3:4], axis=-1, keepdims=True)
    lam = jnp.exp(l01) - jnp.exp(l23) + lam_init

    for sb in range(spb):
        for h in range(H_B):
            ls = slice(h * 128, (h + 1) * 128)
            qn = _rms64(q_ref[sb, :, ls], qg_ref[...], half_sum)
            if latent:
                qn = _rope(qn, cq_ref[...], sq_ref[...])
            qn = qn * (DH_B ** -0.5)
            lane = lax.broadcasted_iota(jnp.int32, qn.shape, 1)
            kf = k_scr[sb, :, ls]
            v1 = v_scr[sb, :, h * 256:(h + 1) * 256]

            def unnormalised(qm, kf=kf, v1=v1):
                s = _dot_nt(qm.astype(BF16), kf)
                e = jnp.exp(s - jnp.max(s, axis=-1, keepdims=True)).astype(BF16)
                oz = _dot(e, v1)
                return oz[:, 0:128], oz[:, 128:256]

            o0, z0 = unnormalised(jnp.where(lane < DH_B, qn, 0.0))
            o1, z1 = unnormalised(jnp.where(lane < DH_B, 0.0, qn))
            o = o0 * (1.0 / z0) - o1 * (lam / z1)
            y = o * lax.rsqrt(jnp.mean(o * o, axis=-1, keepdims=True) + EPS) * sg_ref[...]
            y_ref[sb, :, ls] = y * (1.0 - lam_init)


def _diff_attn(q3, k3, v3, layer, qn_g, kn_g, sub_g, lam_p, rope_tab, cache_k, cache_v, kv_out,
               prev_k=(), prev_v=()):
    b, t, _ = q3.shape
    latent = rope_tab is not None
    tq = 256
    t_ctx = cache_k.shape[2] if latent else 0
    lam_init = 0.8 - 0.6 * math.exp(-0.3 * layer)
    n_prev = len(prev_k)
    w = H_B * 128

    def small(shape):
        return pl.BlockSpec(shape, lambda bi, qi: (0,) * len(shape))

    spb = 1 if latent else ATTN_SEQS_PER_STEP
    seq_blk = pl.BlockSpec((spb, t, w), lambda bi, qi: (bi, 0, 0))
    in_specs = [
        pl.BlockSpec((spb, tq, w), lambda bi, qi: (bi, qi, 0)),
        seq_blk, seq_blk,
        small((1, 128)), small((1, 128)), small((1, 128)), small((4, DH_B)),
    ]
    args = [q3, k3, v3, jnp.tile(qn_g, 2).reshape(1, 128), jnp.tile(kn_g, 2).reshape(1, 128),
            sub_g.reshape(1, 128), lam_p]
    if latent:
        cos_t, sin_t = rope_tab
        in_specs += [
            pl.BlockSpec((tq, 128), lambda bi, qi: (qi, 0)),
            pl.BlockSpec((tq, 128), lambda bi, qi: (qi, 0)),
            small((t, 128)), small((t, 128)),
            pl.BlockSpec((1, 1, t_ctx, w), lambda bi, qi: (bi, layer, 0, 0)),
            pl.BlockSpec((1, 1, t_ctx, w), lambda bi, qi: (bi, layer, 0, 0)),
        ]
        args += [cos_t, sin_t, cos_t, sin_t, cache_k, cache_v]
    in_specs += [seq_blk] * (2 * n_prev)
    args += list(prev_k) + list(prev_v)
    out_specs = [pl.BlockSpec((spb, tq, w), lambda bi, qi: (bi, qi, 0))]
    out_shape = [jax.ShapeDtypeStruct((b, t, w), F32)]
    if kv_out == "stack":
        stack_spec = pl.BlockSpec((spb, n_prev + 1, t, w), lambda bi, qi: (bi, 0, 0, 0))
        out_specs += [stack_spec, stack_spec]
        out_shape += [jax.ShapeDtypeStruct((b, n_prev + 1, t, w), F32)] * 2
    elif kv_out:
        out_specs.append(seq_blk)
        out_shape.append(jax.ShapeDtypeStruct((b, t, w), F32))
    return pl.pallas_call(
        functools.partial(_attn_kernel, t_self=t, t_ctx=t_ctx, lam_init=lam_init,
                          latent=latent, kv_out=kv_out, n_prev=n_prev),
        grid=(b // spb, t // tq),
        in_specs=in_specs,
        out_specs=out_specs,
        out_shape=out_shape,
        scratch_shapes=[pltpu.VMEM((spb, t + t_ctx, w), BF16),
                        pltpu.VMEM((spb, t + t_ctx, 2 * w), BF16)],
        compiler_params=_cparams(("arbitrary", "arbitrary")),
        name="diff_attn",
    )(*args)


def _dft_tables(t):
    def cs(n):
        idx = np.arange(n, dtype=np.int64)
        ang = 2.0 * np.pi * ((idx[:, None] * idx[None, :]) % n).astype(np.float64) / n
        return np.cos(ang), np.sin(ang)

    cc, sc = cs(DG_C)
    ct, st = cs(t)
    chan = jnp.asarray(np.concatenate([cc, sc], axis=1), dtype=F32)
    pos = jnp.asarray(np.concatenate([ct, -st], axis=1), dtype=F32)
    return chan, pos


def _fourier_kernel(u_ref, chan_ref, pos_ref, y_ref, uc_scr, *, t, scale):
    ri = pl.program_id(1)

    spb, _, w = u_ref.shape

    @pl.when(ri == 0)
    def _():
        for sb in range(spb):
            for g in range(G_C):
                ls = slice(g * DG_C, (g + 1) * DG_C)
                cs = slice(sb * w + g * DG_C, sb * w + (g + 1) * DG_C)
                u1 = _dot(u_ref[sb, :, ls].astype(BF16), chan_ref[...].astype(BF16))
                uc_scr[0:t, cs] = u1[:, 0:DG_C].astype(BF16)
                uc_scr[t:2 * t, cs] = u1[:, DG_C:2 * DG_C].astype(BF16)

    f = _dot(pos_ref[...].astype(BF16), uc_scr[...]) * scale
    for sb in range(spb):
        y_ref[sb] = f[:, sb * w:(sb + 1) * w]


def _fourier(pc3):
    b, t, w = pc3.shape
    tr = min(t, 512)
    spb = FOURIER_SEQS_PER_STEP if t // tr == 1 else 1
    chan, pos = _dft_tables(t)
    scale = 1.0 / math.sqrt(t * DG_C)
    return pl.pallas_call(
        functools.partial(_fourier_kernel, t=t, scale=scale),
        grid=(b // spb, t // tr),
        in_specs=[
            pl.BlockSpec((spb, t, w), lambda bi, ri: (bi, 0, 0)),
            _resident((DG_C, 2 * DG_C)),
            pl.BlockSpec((tr, 2 * t), lambda bi, ri: (ri, 0)),
        ],
        out_specs=pl.BlockSpec((spb, tr, w), lambda bi, ri: (bi, ri, 0)),
        out_shape=jax.ShapeDtypeStruct((b, t, w), F32),
        scratch_shapes=[pltpu.VMEM((2 * t, spb * w), BF16)],
        compiler_params=_cparams(("arbitrary", "arbitrary")),
        name="fourier_mix",
    )(pc3, chan, pos)


def _merge_kernel(x_ref, mod_ref, g_ref, oa_ref, ob_ref, oc_ref, *refs):
    n_seg = len(MERGE_SEGS)
    wm_refs = refs[:n_seg]
    wa_ref, wb_ref, wc_ref, wo_ref, o_ref = refs[n_seg:]
    x = x_ref[...]
    h = _modulated_norm(x, g_ref[...], mod_ref)

    def seg(s):
        return _dot(h, wm_refs[s][0])

    ya = (oa_ref[...] * _silu(seg(0))).astype(BF16)
    yb = (ob_ref[...] * _silu(seg(1))).astype(BF16)
    yc = (oc_ref[...] * _silu(seg(2))).astype(BF16)
    parts = []
    for n in range(2):
        cols = slice(n * SEG, (n + 1) * SEG)
        m = jax.nn.sigmoid(seg(3 + n)) * _dot(ya, wa_ref[:, cols])
        m = m + jax.nn.sigmoid(seg(5 + n)) * _dot(yb, wb_ref[:, cols])
        m = m + jax.nn.sigmoid(seg(7 + n)) * _dot(yc, wc_ref[:, cols])
        parts.append(m.astype(BF16))
    merged = jnp.concatenate(parts, axis=1)
    gate = mod_ref[0, 2:3, :]
    o_ref[...] = x + gate * _dot(merged, wo_ref[...])


def _merge(x2, mod3, row_of_tile, norm_g, oa, ob, oc, w_in_bf, layer, wa, wb, wc, wo):
    n = x2.shape[0]
    tm = TOKEN_TILE
    tok = lambda i: (i, 0)
    in_specs = [
        pl.BlockSpec((tm, D_MODEL), tok),
        pl.BlockSpec((1, 3, D_MODEL), lambda i: (row_of_tile(i), 0, 0)),
        _resident((1, D_MODEL)),
        pl.BlockSpec((tm, SEG), tok), pl.BlockSpec((tm, SEG), tok), pl.BlockSpec((tm, SEG), tok),
    ] + _segment_specs(layer, MERGE_SEGS) + [
        _resident((SEG, D_MODEL)), _resident((SEG, D_MODEL)), _resident((SEG, D_MODEL)),
        _resident((D_MODEL, D_MODEL)),
    ]
    return pl.pallas_call(
        _merge_kernel,
        grid=(n // tm,),
        in_specs=in_specs,
        out_specs=pl.BlockSpec((tm, D_MODEL), tok),
        out_shape=jax.ShapeDtypeStruct((n, D_MODEL), F32),
        compiler_params=_cparams(("arbitrary",)),
        name="merge_out",
    )(x2, mod3, norm_g.reshape(1, D_MODEL), oa, ob, oc, *([w_in_bf] * len(MERGE_SEGS)),
      wa, wb, wc, wo)


def _rope_tables(t):
    nf = DH_B // 4
    n_rows = t // GRID_W
    pos_row = jnp.repeat(jnp.arange(n_rows, dtype=F32), GRID_W)
    pos_col = jnp.tile(jnp.arange(GRID_W, dtype=F32), n_rows)
    inv = ROPE_BASE ** (-jnp.arange(nf, dtype=F32) / nf)
    ar = pos_row[:, None] * inv[None, :]
    ac = pos_col[:, None] * inv[None, :]
    cos64 = jnp.concatenate([jnp.cos(ar), jnp.cos(ar), jnp.cos(ac), jnp.cos(ac)], axis=1)
    sin64 = jnp.concatenate([-jnp.sin(ar), jnp.sin(ar), -jnp.sin(ac), jnp.sin(ac)], axis=1)
    return jnp.tile(cos64, (1, 2)), jnp.tile(sin64, (1, 2))


def _layer(x, mod3, row_of_tile, layer, lw, log_lb, log1m_lb, ctx=None, rope_tab=None, prev=None):
    (norm_g, w_in_bf, hgrn_g, qn_g, kn_g, lam_p, sub_g, wa, wb, wc, wo) = lw
    b, t, _ = x.shape
    x2 = x.reshape(b * t, D_MODEL)
    segs = _in_proj(x2, mod3, row_of_tile, norm_g, w_in_bf, layer, log_lb, log1m_lb)
    qa, lff, lfb, ia, qb, kb, vb, uc = (s.reshape(b, t, SEG) for s in segs)
    attn = functools.partial(_diff_attn, qb, kb, vb, layer, qn_g, kn_g, sub_g, lam_p)
    is_ctx = ctx is None
    if not is_ctx:
        oa, _ = _hgrn(qa, lff, lfb, ia, hgrn_g, ctx[2], want_state=False)
        (ob,) = attn(rope_tab, ctx[0], ctx[1], None)
    elif prev is None:
        oa, s_out = _hgrn(qa, lff, lfb, ia, hgrn_g, None, want_state=True)
        ob, k_out = attn(None, None, None, "plain")
        v_out = vb
    else:
        oa, s_out = _hgrn(qa, lff, lfb, ia, hgrn_g, None, want_state=True, prev_states=prev[2])
        ob, k_out, v_out = attn(None, None, None, "stack", prev_k=prev[0], prev_v=prev[1])
    oc = _fourier(uc)
    x_new = _merge(x2, mod3, row_of_tile, norm_g, oa.reshape(b * t, SEG), ob.reshape(b * t, SEG),
                   oc.reshape(b * t, SEG), w_in_bf, layer, wa, wb, wc, wo)
    x_new = x_new.reshape(b, t, D_MODEL)
    if is_ctx:
        return x_new, k_out, v_out, s_out
    return x_new


def kernel(x_prompt, x_sample, c, cache_diff_k, cache_diff_v, state_hgrn, c_ctx, ada_w, ada_b,
           norm_g, w_in, hgrn_lb, hgrn_norm_g, diff_qn_g, diff_kn_g, diff_lambda, diff_subln_g,
           w_branch_a, w_branch_b, w_branch_c, w_out):
    batch, seq, _ = x_prompt.shape
    dec_b, dec_t, _ = x_sample.shape
    past = cache_diff_k.shape[2]

    lb_all = jnp.cumsum(jax.nn.softmax(hgrn_lb.astype(F32), axis=0), axis=0)
    lb_all = lb_all - lb_all[0]
    log_lb = jnp.log(lb_all)
    log1m_lb = jnp.log1p(-lb_all)

    cond8 = jnp.zeros((8, D_MODEL), F32).at[0].set(c_ctx).at[1:1 + dec_b].set(c)
    mod = _modulation(cond8, ada_w, ada_b).reshape(DEPTH, 8, 3, D_MODEL)

    rope_tab = _rope_tables(dec_t)
    cache_k4 = cache_diff_k.reshape(dec_b, DEPTH, past, H_B * 2 * DH_B)
    cache_v4 = cache_diff_v.reshape(dec_b, DEPTH, past, H_B * DV_B)
    tiles_per_seq = dec_t // TOKEN_TILE
    ctx_row = lambda i: 0
    lat_row = lambda i: 1 + i // tiles_per_seq

    xp, xs = x_prompt, x_sample
    prev_k, prev_pb, prev_s = [], [], []
    w_in_bf = w_in.astype(BF16)
    for l in range(DEPTH):
        lw = (norm_g[l], w_in_bf,
              hgrn_norm_g[l], diff_qn_g[l], diff_kn_g[l], diff_lambda[l], diff_subln_g[l],
              w_branch_a[l].astype(BF16), w_branch_b[l].astype(BF16),
              w_branch_c[l].astype(BF16), w_out[l].astype(BF16))
        mod3 = mod[l]
        last = l == DEPTH - 1
        xp, k_c, v_c, s_c = _layer(xp, mod3, ctx_row, l, lw, log_lb[l], log1m_lb[l],
                                   prev=(prev_k, prev_pb, prev_s) if last and l > 0 else None)
        if not last:
            prev_k.append(k_c)
            prev_pb.append(v_c)
            prev_s.append(s_c)
        xs = _layer(xs, mod3, lat_row, l, lw, log_lb[l], log1m_lb[l],
                    ctx=(cache_k4, cache_v4, state_hgrn[:, l]), rope_tab=rope_tab)
    new_k = k_c.reshape(batch, DEPTH, seq, H_B, 2, DH_B)
    new_v = v_c.reshape(batch, DEPTH, seq, H_B, DV_B)
    return (xp, xs, new_k, new_v, s_c)
```

```python
import functools
import math

import numpy as np
import jax
import jax.numpy as jnp
from jax import lax
from jax.experimental import pallas as pl
from jax.experimental.pallas import tpu as pltpu

F32 = jnp.float32
BF16 = jnp.bfloat16

D_MODEL = 1024
DEPTH = 2
GRID_W = 64
EPS = 1e-6
H_A = 4
DK_A = 128
DV_A = 128
H_B = 4
DH_B = 64
DV_B = 128
G_C = 4
DG_C = 128
ROPE_BASE = 10000.0
SEG = 512
(S_QA, S_FFA, S_FBA, S_IA, S_ZA, S_QB, S_KB, S_VB, S_ZB, S_UC, S_ZC) = range(11)
S_GATE = 11
PROJ_SEGS = (S_QA, S_FFA, S_FBA, S_IA, S_QB, S_KB, S_VB, S_UC)
MERGE_SEGS = (S_ZA, S_ZB, S_ZC) + tuple(range(S_GATE, S_GATE + 6))

CHUNK = 128
SUB = 16
HEADS_PER_STEP = 4
TOKEN_TILE = 512
ATTN_SEQS_PER_STEP = 4
ATTN_QUERY_TILE = 512
FOURIER_SEQS_PER_STEP = 4
VMEM_LIMIT = 56 * 1024 * 1024


def _cparams(sem):
    return pltpu.CompilerParams(dimension_semantics=sem, vmem_limit_bytes=VMEM_LIMIT)


def _resident(shape):
    return pl.BlockSpec(shape, lambda *_: (0,) * len(shape), pipeline_mode=pl.Buffered(1))


def _silu(x):
    return x * jax.nn.sigmoid(x)


def _log_sigmoid(x):
    return jnp.minimum(x, 0.0) - jnp.log1p(jnp.exp(-jnp.abs(x)))


def _dot(a, b):
    return jnp.dot(a, b, preferred_element_type=F32)


def _dot_nt(a, b):
    return lax.dot_general(a, b, (((1,), (1,)), ((), ())), preferred_element_type=F32)


def _modulated_norm(x, g, mod_ref):
    y = x * lax.rsqrt(jnp.mean(x * x, axis=-1, keepdims=True) + EPS) * g
    shift = mod_ref[0, 0:1, :]
    scale = mod_ref[0, 1:2, :]
    return (y * (1.0 + scale) + shift).astype(BF16)


def _mod_kernel(c_ref, w_ref, b_ref, o_ref):
    c = c_ref[...]
    s = _silu(c).astype(BF16)
    o_ref[0] = _dot(s, w_ref[0].astype(BF16)) + b_ref[0]


def _modulation(cond8, ada_w, ada_b):
    tn = 512
    return pl.pallas_call(
        _mod_kernel,
        grid=(DEPTH, 3 * D_MODEL // tn),
        in_specs=[
            pl.BlockSpec((8, D_MODEL), lambda l, j: (0, 0)),
            pl.BlockSpec((1, D_MODEL, tn), lambda l, j: (l, 0, j)),
            pl.BlockSpec((1, 1, tn), lambda l, j: (l, 0, j)),
        ],
        out_specs=pl.BlockSpec((1, 8, tn), lambda l, j: (l, 0, j)),
        out_shape=jax.ShapeDtypeStruct((DEPTH, 8, 3 * D_MODEL), F32),
        compiler_params=_cparams(("arbitrary", "arbitrary")),
        name="adaln_mod",
    )(cond8, ada_w, ada_b.reshape(DEPTH, 1, 3 * D_MODEL))


def _proj_kernel(x_ref, mod_ref, g_ref, la_ref, l1_ref, *refs):
    n_seg = len(PROJ_SEGS)
    w_refs, o_refs = refs[:n_seg], refs[n_seg:]
    h = _modulated_norm(x_ref[...], g_ref[...], mod_ref)

    def log_f(raw, d):
        a = la_ref[d:d + 1, :]
        b = l1_ref[d:d + 1, :] + _log_sigmoid(raw)
        return jnp.maximum(a, b) + jnp.log1p(jnp.exp(-jnp.abs(a - b)))

    for s, w_ref, o_ref in zip(PROJ_SEGS, w_refs, o_refs):
        acc = _dot(h, w_ref[0])
        if s == S_QA:
            acc = _silu(acc)
        elif s in (S_FFA, S_FBA):
            acc = log_f(acc, s - S_FFA)
        o_ref[...] = acc


def _segment_specs(layer, segs):
    return [pl.BlockSpec((1, D_MODEL, SEG), (lambda *_, s=s: (layer, 0, s)),
                         pipeline_mode=pl.Buffered(1)) for s in segs]


def _in_proj(x2, mod3, row_of_tile, norm_g, w_in_bf, layer, log_lb, log1m_lb):
    n = x2.shape[0]
    tm = TOKEN_TILE
    tok = lambda i: (i, 0)
    n_seg = len(PROJ_SEGS)
    return pl.pallas_call(
        _proj_kernel,
        grid=(n // tm,),
        in_specs=[
            pl.BlockSpec((tm, D_MODEL), tok),
            pl.BlockSpec((1, 3, D_MODEL), lambda i: (row_of_tile(i), 0, 0)),
            _resident((1, D_MODEL)),
            _resident((2, SEG)),
            _resident((2, SEG)),
        ] + _segment_specs(layer, PROJ_SEGS),
        out_specs=[pl.BlockSpec((tm, SEG), tok)] * n_seg,
        out_shape=[jax.ShapeDtypeStruct((n, SEG), F32)] * n_seg,
        compiler_params=_cparams(("arbitrary",)),
        name="in_proj",
    )(x2, mod3, norm_g.reshape(1, D_MODEL), log_lb, log1m_lb, *([w_in_bf] * n_seg))


def _split3(x):
    hi = x.astype(BF16)
    r = x - hi.astype(F32)
    mid = r.astype(BF16)
    lo = (r - mid.astype(F32)).astype(BF16)
    return hi, mid, lo


def _level_factors(q, k, g, m, reverse, diag):
    nb = CHUNK // m
    fq, fk = [], []
    zeros = jnp.zeros((m, 128), BF16)
    for i in range(nb):
        rows = slice(i * m, (i + 1) * m)
        sl = g[rows]

        def scaled(x, e):
            return (x[rows] * jnp.exp(e)).astype(BF16)

        if diag:
            r = i * m + (m // 2 - 1 if not reverse else m // 2)
            ref = g[r:r + 1]
            fq.append(scaled(q, sl - ref))
            fk.append(scaled(k, ref - sl))
        elif (i % 2 == 1) != reverse:
            r = (i + 1) * m if reverse else i * m - 1
            fq.append(scaled(q, sl - g[r:r + 1]))
            fk.append(zeros)
        else:
            r = i * m if reverse else (i + 1) * m - 1
            fq.append(zeros)
            fk.append(scaled(k, g[r:r + 1] - sl))
    return jnp.concatenate(fq, axis=0), jnp.concatenate(fk, axis=0)


def _level_masks(reverse):
    t = lax.broadcasted_iota(jnp.int32, (CHUNK, CHUNK), 0)
    s = lax.broadcasted_iota(jnp.int32, (CHUNK, CHUNK), 1)
    sh = int(math.log2(SUB))
    same = (t >> sh) == (s >> sh)
    masks = [same & ((s >= t) if reverse else (s <= t))]
    m = SUB
    while m < CHUNK:
        sh = int(math.log2(m))
        tb, sb = t >> sh, s >> sh
        if not reverse:
            masks.append(((tb & 1) == 1) & (sb == tb - 1))
        else:
            masks.append(((tb & 1) == 0) & (sb == tb + 1))
        m *= 2
    return masks


def _hgrn_chunk(lf, q, v, st, tri, masks, reverse):
    end_row = 0 if reverse else CHUNK - 1
    k = 1.0 - jnp.exp(lf)
    g = _dot(tri, jnp.concatenate(_split3(lf), axis=0))
    g_end = g[end_row:end_row + 1]
    vt_bf = v.T.astype(BF16)

    a = jnp.zeros((CHUNK, CHUNK), F32)
    m = SUB
    for lvl, mask in enumerate(masks):
        fq, fk = _level_factors(q, k, g, m, reverse, diag=(lvl == 0))
        a = jnp.where(mask, _dot_nt(fq, fk), a)
        if lvl > 0:
            m *= 2

    qg = (q * jnp.exp(g)).astype(BF16)
    o = _dot_nt(jnp.concatenate([qg, a.astype(BF16)], axis=1),
                jnp.concatenate([st.astype(BF16), vt_bf], axis=1))

    kg = (k * jnp.exp(g_end - g)).astype(BF16)
    u_t = _dot(vt_bf, kg)
    return o, jnp.exp(g_end) * st + u_t


def _hgrn_kernel(*refs, n_chunks, has_state, want_state, n_prev):
    it = iter(refs)
    q_ref, lff_ref, lfb_ref, v_ref, g_ref = (next(it) for _ in range(5))
    s0_ref = next(it) if has_state else None
    prev_refs = [next(it) for _ in range(n_prev)]
    y_ref = next(it)
    sfin_ref = next(it) if want_state else None
    o_scr, st_scr = next(it), next(it)
    hpb = HEADS_PER_STEP

    t_i = lax.broadcasted_iota(jnp.int32, (CHUNK, CHUNK), 0)
    s_i = lax.broadcasted_iota(jnp.int32, (CHUNK, CHUNK), 1)
    tris = [jnp.where(s_i <= t_i, 1.0, 0.0).astype(BF16), jnp.where(s_i >= t_i, 1.0, 0.0).astype(BF16)]
    tris = [jnp.concatenate([tr, tr, tr], axis=1) for tr in tris]
    masks = [_level_masks(False), _level_masks(True)]
    lf_refs = (lff_ref, lfb_ref)

    for hh in range(hpb):
        for d in range(2):
            if has_state:
                st_scr[2 * hh + d] = s0_ref[0, d, hh].T
            else:
                st_scr[2 * hh + d] = jnp.zeros((DV_A, DK_A), F32)

    def step(ci):
        for hh in range(hpb):
            ls = slice(hh * 128, (hh + 1) * 128)
            for d in range(2):
                c = ci if d == 0 else n_chunks - 1 - ci
                r0 = c * CHUNK
                if not isinstance(r0, int):
                    r0 = pl.multiple_of(r0, CHUNK)
                rows = pl.ds(r0, CHUNK)
                o, st = _hgrn_chunk(lf_refs[d][0, rows, ls], q_ref[0, rows, ls], v_ref[0, rows, ls],
                                    st_scr[2 * hh + d], tris[d], masks[d], reverse=(d == 1))
                o_scr[d, rows, ls] = o
                st_scr[2 * hh + d] = st

    if n_chunks <= 2:
        for ci in range(n_chunks):
            step(ci)
    else:
        def body(i, carry):
            step(2 * i)
            step(2 * i + 1)
            return carry
        lax.fori_loop(0, n_chunks // 2, body, 0)

    for j, p_ref in enumerate(prev_refs):
        sfin_ref[0, j] = p_ref[0]
    for hh in range(hpb):
        ls = slice(hh * 128, (hh + 1) * 128)
        if want_state:
            for d in range(2):
                if n_prev:
                    sfin_ref[0, n_prev, d, hh] = st_scr[2 * hh + d].T
                else:
                    sfin_ref[0, d, hh] = st_scr[2 * hh + d].T
        o = o_scr[0, :, ls] + o_scr[1, :, ls]
        y_ref[0, :, ls] = o * lax.rsqrt(jnp.mean(o * o, axis=-1, keepdims=True) + EPS) * g_ref[0, :, ls]


def _hgrn(q3, lff3, lfb3, i3, hgrn_g, s0, want_state, prev_states=()):
    b, t, _ = q3.shape
    n_chunks = t // CHUNK
    has_state = s0 is not None
    hpb = HEADS_PER_STEP
    w = 128 * hpb
    n_prev = len(prev_states)
    col = pl.BlockSpec((1, t, w), lambda bi, hp: (bi, 0, hp))

    in_specs = [col, col, col, col, pl.BlockSpec((1, 1, w), lambda bi, hp: (hp, 0, 0))]
    args = [q3, lff3, lfb3, i3, hgrn_g.reshape(H_A // hpb, 1, w)]
    st_spec = pl.BlockSpec((1, 2, hpb, DK_A, DV_A), lambda bi, hp: (bi, 0, hp, 0, 0))
    if has_state:
        in_specs.append(st_spec)
        args.append(s0)
    in_specs += [st_spec] * n_prev
    args += list(prev_states)
    out_specs = [pl.BlockSpec((1, t, w), lambda bi, hp: (bi, 0, hp))]
    out_shape = [jax.ShapeDtypeStruct((b, t, H_A * DV_A), F32)]
    if want_state and n_prev:
        out_specs.append(pl.BlockSpec((1, n_prev + 1, 2, hpb, DK_A, DV_A),
                                      lambda bi, hp: (bi, 0, 0, hp, 0, 0)))
        out_shape.append(jax.ShapeDtypeStruct((b, n_prev + 1, 2, H_A, DK_A, DV_A), F32))
    elif want_state:
        out_specs.append(st_spec)
        out_shape.append(jax.ShapeDtypeStruct((b, 2, H_A, DK_A, DV_A), F32))
    res = pl.pallas_call(
        functools.partial(_hgrn_kernel, n_chunks=n_chunks, has_state=has_state,
                          want_state=want_state, n_prev=n_prev),
        grid=(b, H_A // hpb),
        in_specs=in_specs,
        out_specs=out_specs,
        out_shape=out_shape,
        scratch_shapes=[pltpu.VMEM((2, t, w), F32), pltpu.VMEM((2 * hpb, DV_A, DK_A), F32)],
        compiler_params=_cparams(("arbitrary", "arbitrary")),
        name="hgrn2",
    )(*args)
    return res if want_state else (res[0], None)


def _half_sum_matrix():
    r = lax.broadcasted_iota(jnp.int32, (128, 128), 0)
    c = lax.broadcasted_iota(jnp.int32, (128, 128), 1)
    return jnp.where((r >> 6) == (c >> 6), 1.0, 0.0).astype(BF16)


def _rms64(x, g, half_sum):
    x2 = x * x
    hi = x2.astype(BF16)
    lo = (x2 - hi.astype(F32)).astype(BF16)
    ss = _dot(hi, half_sum) + _dot(lo, half_sum)
    return x * lax.rsqrt(ss / DH_B + EPS) * g


def _rope(x, cos, sin):
    lane = lax.broadcasted_iota(jnp.int32, x.shape, 1)
    first = (lane & 31) < 16
    partner = jnp.where(first, pltpu.roll(x, 112, 1), pltpu.roll(x, 16, 1))
    return x * cos + partner * sin


def _attn_kernel(*refs, t_self, t_ctx, lam_init, latent, kv_out, n_prev):
    it = iter(refs)
    q_ref, k_ref, v_ref = (next(it) for _ in range(3))
    qg_ref, kg_ref, sg_ref, lam_ref = (next(it) for _ in range(4))
    if latent:
        cq_ref, sq_ref, ck_ref, sk_ref, kc_ref, vc_ref = (next(it) for _ in range(6))
    prev_k = [next(it) for _ in range(n_prev)]
    prev_v = [next(it) for _ in range(n_prev)]
    y_ref = next(it)
    kout_ref = next(it) if kv_out else None
    vout_ref = next(it) if kv_out == "stack" else None
    k_scr, v_scr = next(it), next(it)

    qi = pl.program_id(1)
    half_sum = _half_sum_matrix()
    t_all = t_self + t_ctx
    spb = q_ref.shape[0]

    @pl.when(qi == 0)
    def _():
        for sb in range(spb):
            for j in range(n_prev):
                kout_ref[sb, j] = prev_k[j][sb]
                vout_ref[sb, j] = prev_v[j][sb]
            if kv_out == "stack":
                vout_ref[sb, n_prev] = v_ref[sb]
            for h in range(H_B):
                ls = slice(h * 128, (h + 1) * 128)
                vs = slice(h * 256, h * 256 + 128)
                kn = _rms64(k_ref[sb, :, ls], kg_ref[...], half_sum)
                if kv_out == "stack":
                    kout_ref[sb, n_prev, :, ls] = kn
                elif kv_out:
                    kout_ref[sb, :, ls] = kn
                if latent:
                    kn = _rope(kn, ck_ref[...], sk_ref[...])
                    k_scr[sb, t_self:t_all, ls] = kc_ref[sb, 0, :, ls].astype(BF16)
                    v_scr[sb, t_self:t_all, vs] = vc_ref[sb, 0, :, ls].astype(BF16)
                k_scr[sb, 0:t_self, ls] = kn.astype(BF16)
                v_scr[sb, 0:t_self, vs] = v_ref[sb, :, ls].astype(BF16)
                v_scr[sb, :, h * 256 + 128:(h + 1) * 256] = jnp.ones((t_all, 128), BF16)

    lp = lam_ref[...]
    l01 = jnp.sum(lp[0:1] * lp[1:2], axis=-1, keepdims=True)
    l23 = jnp.sum(lp[2:3] * lp[3:4], axis=-1, keepdims=True)
    lam = jnp.exp(l01) - jnp.exp(l23) + lam_init

    for sb in range(spb):
        for h in range(H_B):
            ls = slice(h * 128, (h + 1) * 128)
            qn = _rms64(q_ref[sb, :, ls], qg_ref[...], half_sum)
            if latent:
                qn = _rope(qn, cq_ref[...], sq_ref[...])
            qn = qn * (DH_B ** -0.5)
            lane = lax.broadcasted_iota(jnp.int32, qn.shape, 1)
            kf = k_scr[sb, :, ls]
            v1 = v_scr[sb, :, h * 256:(h + 1) * 256]

            tq = qn.shape[0]
            qm = jnp.concatenate([jnp.where(lane < DH_B, qn, 0.0), jnp.where(lane < DH_B, 0.0, qn)],
                                 axis=0).astype(BF16)
            s = _dot_nt(qm, kf)
            e = jnp.exp(s - jnp.max(s, axis=-1, keepdims=True)).astype(BF16)
            oz = _dot(e, v1)
            o0, z0 = oz[0:tq, 0:128], oz[0:tq, 128:256]
            o1, z1 = oz[tq:2 * tq, 0:128], oz[tq:2 * tq, 128:256]
            o = o0 * (1.0 / z0) - o1 * (lam / z1)
            y = o * lax.rsqrt(jnp.mean(o * o, axis=-1, keepdims=True) + EPS) * sg_ref[...]
            y_ref[sb, :, ls] = y * (1.0 - lam_init)


def _diff_attn(q3, k3, v3, layer, qn_g, kn_g, sub_g, lam_p, rope_tab, cache_k, cache_v, kv_out,
               prev_k=(), prev_v=()):
    b, t, _ = q3.shape
    latent = rope_tab is not None
    tq = min(t, ATTN_QUERY_TILE)
    t_ctx = cache_k.shape[2] if latent else 0
    lam_init = 0.8 - 0.6 * math.exp(-0.3 * layer)
    n_prev = len(prev_k)
    w = H_B * 128

    def small(shape):
        return pl.BlockSpec(shape, lambda bi, qi: (0,) * len(shape))

    spb = 1 if latent else ATTN_SEQS_PER_STEP
    seq_blk = pl.BlockSpec((spb, t, w), lambda bi, qi: (bi, 0, 0))
    in_specs = [
        pl.BlockSpec((spb, tq, w), lambda bi, qi: (bi, qi, 0)),
        seq_blk, seq_blk,
        small((1, 128)), small((1, 128)), small((1, 128)), small((4, DH_B)),
    ]
    args = [q3, k3, v3, jnp.tile(qn_g, 2).reshape(1, 128), jnp.tile(kn_g, 2).reshape(1, 128),
            sub_g.reshape(1, 128), lam_p]
    if latent:
        cos_t, sin_t = rope_tab
        in_specs += [
            pl.BlockSpec((tq, 128), lambda bi, qi: (qi, 0)),
            pl.BlockSpec((tq, 128), lambda bi, qi: (qi, 0)),
            small((t, 128)), small((t, 128)),
            pl.BlockSpec((1, 1, t_ctx, w), lambda bi, qi: (bi, layer, 0, 0)),
            pl.BlockSpec((1, 1, t_ctx, w), lambda bi, qi: (bi, layer, 0, 0)),
        ]
        args += [cos_t, sin_t, cos_t, sin_t, cache_k, cache_v]
    in_specs += [seq_blk] * (2 * n_prev)
    args += list(prev_k) + list(prev_v)
    out_specs = [pl.BlockSpec((spb, tq, w), lambda bi, qi: (bi, qi, 0))]
    out_shape = [jax.ShapeDtypeStruct((b, t, w), F32)]
    if kv_out == "stack":
        stack_spec = pl.BlockSpec((spb, n_prev + 1, t, w), lambda bi, qi: (bi, 0, 0, 0))
        out_specs += [stack_spec, stack_spec]
        out_shape += [jax.ShapeDtypeStruct((b, n_prev + 1, t, w), F32)] * 2
    elif kv_out:
        out_specs.append(seq_blk)
        out_shape.append(jax.ShapeDtypeStruct((b, t, w), F32))
    return pl.pallas_call(
        functools.partial(_attn_kernel, t_self=t, t_ctx=t_ctx, lam_init=lam_init,
                          latent=latent, kv_out=kv_out, n_prev=n_prev),
        grid=(b // spb, t // tq),
        in_specs=in_specs,
        out_specs=out_specs,
        out_shape=out_shape,
        scratch_shapes=[pltpu.VMEM((spb, t + t_ctx, w), BF16),
                        pltpu.VMEM((spb, t + t_ctx, 2 * w), BF16)],
        compiler_params=_cparams(("arbitrary", "arbitrary")),
        name="diff_attn",
    )(*args)


def _dft_tables(t):
    def cs(n):
        idx = np.arange(n, dtype=np.int64)
        ang = 2.0 * np.pi * ((idx[:, None] * idx[None, :]) % n).astype(np.float64) / n
        return np.cos(ang), np.sin(ang)

    cc, sc = cs(DG_C)
    ct, st = cs(t)
    chan = jnp.asarray(np.concatenate([cc, sc], axis=1), dtype=F32)
    pos = jnp.asarray(np.concatenate([ct, -st], axis=1), dtype=F32)
    return chan, pos


def _fourier_kernel(u_ref, chan_ref, pos_ref, y_ref, uc_scr, *, t, scale):
    ri = pl.program_id(1)

    spb, _, w = u_ref.shape

    @pl.when(ri == 0)
    def _():
        for sb in range(spb):
            for g in range(G_C):
                ls = slice(g * DG_C, (g + 1) * DG_C)
                cs = slice(sb * w + g * DG_C, sb * w + (g + 1) * DG_C)
                u1 = _dot(u_ref[sb, :, ls].astype(BF16), chan_ref[...].astype(BF16))
                uc_scr[0:t, cs] = u1[:, 0:DG_C].astype(BF16)
                uc_scr[t:2 * t, cs] = u1[:, DG_C:2 * DG_C].astype(BF16)

    f = _dot(pos_ref[...].astype(BF16), uc_scr[...]) * scale
    for sb in range(spb):
        y_ref[sb] = f[:, sb * w:(sb + 1) * w]


def _fourier(pc3):
    b, t, w = pc3.shape
    tr = min(t, 512)
    spb = FOURIER_SEQS_PER_STEP if t // tr == 1 else 1
    chan, pos = _dft_tables(t)
    scale = 1.0 / math.sqrt(t * DG_C)
    return pl.pallas_call(
        functools.partial(_fourier_kernel, t=t, scale=scale),
        grid=(b // spb, t // tr),
        in_specs=[
            pl.BlockSpec((spb, t, w), lambda bi, ri: (bi, 0, 0)),
            _resident((DG_C, 2 * DG_C)),
            pl.BlockSpec((tr, 2 * t), lambda bi, ri: (ri, 0)),
        ],
        out_specs=pl.BlockSpec((spb, tr, w), lambda bi, ri: (bi, ri, 0)),
        out_shape=jax.ShapeDtypeStruct((b, t, w), F32),
        scratch_shapes=[pltpu.VMEM((2 * t, spb * w), BF16)],
        compiler_params=_cparams(("arbitrary", "arbitrary")),
        name="fourier_mix",
    )(pc3, chan, pos)


def _merge_kernel(x_ref, mod_ref, g_ref, oa_ref, ob_ref, oc_ref, *refs):
    n_seg = len(MERGE_SEGS)
    wm_refs = refs[:n_seg]
    wa_ref, wb_ref, wc_ref, wo_ref, o_ref = refs[n_seg:]
    x = x_ref[...]
    h = _modulated_norm(x, g_ref[...], mod_ref)

    def seg(s):
        return _dot(h, wm_refs[s][0])

    ya = (oa_ref[...] * _silu(seg(0))).astype(BF16)
    yb = (ob_ref[...] * _silu(seg(1))).astype(BF16)
    yc = (oc_ref[...] * _silu(seg(2))).astype(BF16)
    parts = []
    for n in range(2):
        cols = slice(n * SEG, (n + 1) * SEG)
        m = jax.nn.sigmoid(seg(3 + n)) * _dot(ya, wa_ref[:, cols])
        m = m + jax.nn.sigmoid(seg(5 + n)) * _dot(yb, wb_ref[:, cols])
        m = m + jax.nn.sigmoid(seg(7 + n)) * _dot(yc, wc_ref[:, cols])
        parts.append(m.astype(BF16))
    merged = jnp.concatenate(parts, axis=1)
    gate = mod_ref[0, 2:3, :]
    o_ref[...] = x + gate * _dot(merged, wo_ref[...])


def _merge(x2, mod3, row_of_tile, norm_g, oa, ob, oc, w_in_bf, layer, wa, wb, wc, wo):
    n = x2.shape[0]
    tm = TOKEN_TILE
    tok = lambda i: (i, 0)
    in_specs = [
        pl.BlockSpec((tm, D_MODEL), tok),
        pl.BlockSpec((1, 3, D_MODEL), lambda i: (row_of_tile(i), 0, 0)),
        _resident((1, D_MODEL)),
        pl.BlockSpec((tm, SEG), tok), pl.BlockSpec((tm, SEG), tok), pl.BlockSpec((tm, SEG), tok),
    ] + _segment_specs(layer, MERGE_SEGS) + [
        _resident((SEG, D_MODEL)), _resident((SEG, D_MODEL)), _resident((SEG, D_MODEL)),
        _resident((D_MODEL, D_MODEL)),
    ]
    return pl.pallas_call(
        _merge_kernel,
        grid=(n // tm,),
        in_specs=in_specs,
        out_specs=pl.BlockSpec((tm, D_MODEL), tok),
        out_shape=jax.ShapeDtypeStruct((n, D_MODEL), F32),
        compiler_params=_cparams(("arbitrary",)),
        name="merge_out",
    )(x2, mod3, norm_g.reshape(1, D_MODEL), oa, ob, oc, *([w_in_bf] * len(MERGE_SEGS)),
      wa, wb, wc, wo)


def _rope_tables(t):
    nf = DH_B // 4
    n_rows = t // GRID_W
    pos_row = jnp.repeat(jnp.arange(n_rows, dtype=F32), GRID_W)
    pos_col = jnp.tile(jnp.arange(GRID_W, dtype=F32), n_rows)
    inv = ROPE_BASE ** (-jnp.arange(nf, dtype=F32) / nf)
    ar = pos_row[:, None] * inv[None, :]
    ac = pos_col[:, None] * inv[None, :]
    cos64 = jnp.concatenate([jnp.cos(ar), jnp.cos(ar), jnp.cos(ac), jnp.cos(ac)], axis=1)
    sin64 = jnp.concatenate([-jnp.sin(ar), jnp.sin(ar), -jnp.sin(ac), jnp.sin(ac)], axis=1)
    return jnp.tile(cos64, (1, 2)), jnp.tile(sin64, (1, 2))


def _layer(x, mod3, row_of_tile, layer, lw, log_lb, log1m_lb, ctx=None, rope_tab=None, prev=None):
    (norm_g, w_in_bf, hgrn_g, qn_g, kn_g, lam_p, sub_g, wa, wb, wc, wo) = lw
    b, t, _ = x.shape
    x2 = x.reshape(b * t, D_MODEL)
    segs = _in_proj(x2, mod3, row_of_tile, norm_g, w_in_bf, layer, log_lb, log1m_lb)
    qa, lff, lfb, ia, qb, kb, vb, uc = (s.reshape(b, t, SEG) for s in segs)
    attn = functools.partial(_diff_attn, qb, kb, vb, layer, qn_g, kn_g, sub_g, lam_p)
    is_ctx = ctx is None
    if not is_ctx:
        oa, _ = _hgrn(qa, lff, lfb, ia, hgrn_g, ctx[2], want_state=False)
        (ob,) = attn(rope_tab, ctx[0], ctx[1], None)
    elif prev is None:
        oa, s_out = _hgrn(qa, lff, lfb, ia, hgrn_g, None, want_state=True)
        ob, k_out = attn(None, None, None, "plain")
        v_out = vb
    else:
        oa, s_out = _hgrn(qa, lff, lfb, ia, hgrn_g, None, want_state=True, prev_states=prev[2])
        ob, k_out, v_out = attn(None, None, None, "stack", prev_k=prev[0], prev_v=prev[1])
    oc = _fourier(uc)
    x_new = _merge(x2, mod3, row_of_tile, norm_g, oa.reshape(b * t, SEG), ob.reshape(b * t, SEG),
                   oc.reshape(b * t, SEG), w_in_bf, layer, wa, wb, wc, wo)
    x_new = x_new.reshape(b, t, D_MODEL)
    if is_ctx:
        return x_new, k_out, v_out, s_out
    return x_new


def kernel(x_prompt, x_sample, c, cache_diff_k, cache_diff_v, state_hgrn, c_ctx, ada_w, ada_b,
           norm_g, w_in, hgrn_lb, hgrn_norm_g, diff_qn_g, diff_kn_g, diff_lambda, diff_subln_g,
           w_branch_a, w_branch_b, w_branch_c, w_out):
    batch, seq, _ = x_prompt.shape
    dec_b, dec_t, _ = x_sample.shape
    past = cache_diff_k.shape[2]

    lb_all = jnp.cumsum(jax.nn.softmax(hgrn_lb.astype(F32), axis=0), axis=0)
    lb_all = lb_all - lb_all[0]
    log_lb = jnp.log(lb_all)
    log1m_lb = jnp.log1p(-lb_all)

    cond8 = jnp.zeros((8, D_MODEL), F32).at[0].set(c_ctx).at[1:1 + dec_b].set(c)
    mod = _modulation(cond8, ada_w, ada_b).reshape(DEPTH, 8, 3, D_MODEL)

    rope_tab = _rope_tables(dec_t)
    cache_k4 = cache_diff_k.reshape(dec_b, DEPTH, past, H_B * 2 * DH_B)
    cache_v4 = cache_diff_v.reshape(dec_b, DEPTH, past, H_B * DV_B)
    tiles_per_seq = dec_t // TOKEN_TILE
    ctx_row = lambda i: 0
    lat_row = lambda i: 1 + i // tiles_per_seq

    xp, xs = x_prompt, x_sample
    prev_k, prev_pb, prev_s = [], [], []
    w_in_bf = w_in.astype(BF16)
    for l in range(DEPTH):
        lw = (norm_g[l], w_in_bf,
              hgrn_norm_g[l], diff_qn_g[l], diff_kn_g[l], diff_lambda[l], diff_subln_g[l],
              w_branch_a[l].astype(BF16), w_branch_b[l].astype(BF16),
              w_branch_c[l].astype(BF16), w_out[l].astype(BF16))
        mod3 = mod[l]
        last = l == DEPTH - 1
        xp, k_c, v_c, s_c = _layer(xp, mod3, ctx_row, l, lw, log_lb[l], log1m_lb[l],
                                   prev=(prev_k, prev_pb, prev_s) if last and l > 0 else None)
        if not last:
            prev_k.append(k_c)
            prev_pb.append(v_c)
            prev_s.append(s_c)
        xs = _layer(xs, mod3, lat_row, l, lw, log_lb[l], log1m_lb[l],
                    ctx=(cache_k4, cache_v4, state_hgrn[:, l]), rope_tab=rope_tab)
    new_k = k_c.reshape(batch, DEPTH, seq, H_B, 2, DH_B)
    new_v = v_c.reshape(batch, DEPTH, seq, H_B, DV_B)
    return (xp, xs, new_k, new_v, s_c)
```

```python
import functools
import math

import numpy as np
import jax
import jax.numpy as jnp
from jax import lax
from jax.experimental import pallas as pl
from jax.experimental.pallas import tpu as pltpu

F32 = jnp.float32
BF16 = jnp.bfloat16

D_MODEL = 1024
DEPTH = 2
GRID_W = 64
EPS = 1e-6
H_A = 4
DK_A = 128
DV_A = 128
H_B = 4
DH_B = 64
DV_B = 128
G_C = 4
DG_C = 128
ROPE_BASE = 10000.0
SEG = 512
(S_QA, S_FFA, S_FBA, S_IA, S_ZA, S_QB, S_KB, S_VB, S_ZB, S_UC, S_ZC) = range(11)
S_GATE = 11
PROJ_SEGS = (S_QA, S_FFA, S_FBA, S_IA, S_QB, S_KB, S_VB, S_UC)
MERGE_SEGS = (S_ZA, S_ZB, S_ZC) + tuple(range(S_GATE, S_GATE + 6))

CHUNK = 128
SUB = 16
HEADS_PER_STEP = 4
TOKEN_TILE = 512
ATTN_SEQS_PER_STEP = 4
ATTN_QUERY_TILE = 512
FOURIER_SEQS_PER_STEP = 4
VMEM_LIMIT = 56 * 1024 * 1024


def _cparams(sem):
    return pltpu.CompilerParams(dimension_semantics=sem, vmem_limit_bytes=VMEM_LIMIT)


def _resident(shape):
    return pl.BlockSpec(shape, lambda *_: (0,) * len(shape), pipeline_mode=pl.Buffered(1))


def _silu(x):
    return x * jax.nn.sigmoid(x)


def _log_sigmoid(x):
    return jnp.minimum(x, 0.0) - jnp.log1p(jnp.exp(-jnp.abs(x)))


def _dot(a, b):
    return jnp.dot(a, b, preferred_element_type=F32)


def _dot_nt(a, b):
    return lax.dot_general(a, b, (((1,), (1,)), ((), ())), preferred_element_type=F32)


def _modulated_norm(x, g, mod_ref):
    y = x * lax.rsqrt(jnp.mean(x * x, axis=-1, keepdims=True) + EPS) * g
    shift = mod_ref[0, 0:1, :]
    scale = mod_ref[0, 1:2, :]
    return (y * (1.0 + scale) + shift).astype(BF16)


def _mod_kernel(c_ref, w_ref, b_ref, o_ref):
    c = c_ref[...]
    s = _silu(c).astype(BF16)
    o_ref[0] = _dot(s, w_ref[0].astype(BF16)) + b_ref[0]


def _modulation(cond8, ada_w, ada_b):
    tn = 512
    return pl.pallas_call(
        _mod_kernel,
        grid=(DEPTH, 3 * D_MODEL // tn),
        in_specs=[
            pl.BlockSpec((8, D_MODEL), lambda l, j: (0, 0)),
            pl.BlockSpec((1, D_MODEL, tn), lambda l, j: (l, 0, j)),
            pl.BlockSpec((1, 1, tn), lambda l, j: (l, 0, j)),
        ],
        out_specs=pl.BlockSpec((1, 8, tn), lambda l, j: (l, 0, j)),
        out_shape=jax.ShapeDtypeStruct((DEPTH, 8, 3 * D_MODEL), F32),
        compiler_params=_cparams(("arbitrary", "arbitrary")),
        name="adaln_mod",
    )(cond8, ada_w, ada_b.reshape(DEPTH, 1, 3 * D_MODEL))


def _proj_kernel(x_ref, mod_ref, g_ref, la_ref, l1_ref, *refs):
    n_seg = len(PROJ_SEGS)
    w_refs, o_refs = refs[:n_seg], refs[n_seg:]
    h = _modulated_norm(x_ref[...], g_ref[...], mod_ref)

    def log_f(raw, d):
        a = la_ref[d:d + 1, :]
        b = l1_ref[d:d + 1, :] + _log_sigmoid(raw)
        return jnp.maximum(a, b) + jnp.log1p(jnp.exp(-jnp.abs(a - b)))

    for s, w_ref, o_ref in zip(PROJ_SEGS, w_refs, o_refs):
        acc = _dot(h, w_ref[0])
        if s == S_QA:
            acc = _silu(acc)
        elif s in (S_FFA, S_FBA):
            acc = log_f(acc, s - S_FFA)
        o_ref[...] = acc


def _segment_specs(layer, segs):
    return [pl.BlockSpec((1, D_MODEL, SEG), (lambda *_, s=s: (layer, 0, s)),
                         pipeline_mode=pl.Buffered(1)) for s in segs]


def _in_proj(x2, mod3, row_of_tile, norm_g, w_in_bf, layer, log_lb, log1m_lb):
    n = x2.shape[0]
    tm = TOKEN_TILE
    tok = lambda i: (i, 0)
    n_seg = len(PROJ_SEGS)
    return pl.pallas_call(
        _proj_kernel,
        grid=(n // tm,),
        in_specs=[
            pl.BlockSpec((tm, D_MODEL), tok),
            pl.BlockSpec((1, 3, D_MODEL), lambda i: (row_of_tile(i), 0, 0)),
            _resident((1, D_MODEL)),
            _resident((2, SEG)),
            _resident((2, SEG)),
        ] + _segment_specs(layer, PROJ_SEGS),
        out_specs=[pl.BlockSpec((tm, SEG), tok)] * n_seg,
        out_shape=[jax.ShapeDtypeStruct((n, SEG), F32)] * n_seg,
        compiler_params=_cparams(("arbitrary",)),
        name="in_proj",
    )(x2, mod3, norm_g.reshape(1, D_MODEL), log_lb, log1m_lb, *([w_in_bf] * n_seg))


def _split3(x):
    hi = x.astype(BF16)
    r = x - hi.astype(F32)
    mid = r.astype(BF16)
    lo = (r - mid.astype(F32)).astype(BF16)
    return hi, mid, lo


def _level_factors(q, k, g, m, reverse, diag):
    nb = CHUNK // m
    fq, fk = [], []
    zeros = jnp.zeros((m, 128), BF16)
    for i in range(nb):
        rows = slice(i * m, (i + 1) * m)
        sl = g[rows]

        def scaled(x, e):
            return (x[rows] * jnp.exp(e)).astype(BF16)

        if diag:
            r = i * m + (m // 2 - 1 if not reverse else m // 2)
            ref = g[r:r + 1]
            fq.append(scaled(q, sl - ref))
            fk.append(scaled(k, ref - sl))
        elif (i % 2 == 1) != reverse:
            r = (i + 1) * m if reverse else i * m - 1
            fq.append(scaled(q, sl - g[r:r + 1]))
            fk.append(zeros)
        else:
            r = i * m if reverse else (i + 1) * m - 1
            fq.append(zeros)
            fk.append(scaled(k, g[r:r + 1] - sl))
    return jnp.concatenate(fq, axis=0), jnp.concatenate(fk, axis=0)


def _level_masks(reverse):
    t = lax.broadcasted_iota(jnp.int32, (CHUNK, CHUNK), 0)
    s = lax.broadcasted_iota(jnp.int32, (CHUNK, CHUNK), 1)
    sh = int(math.log2(SUB))
    same = (t >> sh) == (s >> sh)
    masks = [same & ((s >= t) if reverse else (s <= t))]
    m = SUB
    while m < CHUNK:
        sh = int(math.log2(m))
        tb, sb = t >> sh, s >> sh
        if not reverse:
            masks.append(((tb & 1) == 1) & (sb == tb - 1))
        else:
            masks.append(((tb & 1) == 0) & (sb == tb + 1))
        m *= 2
    return masks


def _hgrn_chunk(lf, q, v, st, tri, masks, reverse):
    end_row = 0 if reverse else CHUNK - 1
    k = 1.0 - jnp.exp(lf)
    g = _dot(tri, jnp.concatenate(_split3(lf), axis=0))
    g_end = g[end_row:end_row + 1]
    vt_bf = v.T.astype(BF16)

    a = jnp.zeros((CHUNK, CHUNK), F32)
    m = SUB
    for lvl, mask in enumerate(masks):
        fq, fk = _level_factors(q, k, g, m, reverse, diag=(lvl == 0))
        a = jnp.where(mask, _dot_nt(fq, fk), a)
        if lvl > 0:
            m *= 2

    qg = (q * jnp.exp(g)).astype(BF16)
    o = _dot_nt(jnp.concatenate([qg, a.astype(BF16)], axis=1),
                jnp.concatenate([st.astype(BF16), vt_bf], axis=1))

    kg = (k * jnp.exp(g_end - g)).astype(BF16)
    u_t = _dot(vt_bf, kg)
    return o, jnp.exp(g_end) * st + u_t


def _hgrn_kernel(*refs, n_chunks, has_state, want_state, n_prev):
    it = iter(refs)
    q_ref, lff_ref, lfb_ref, v_ref, g_ref = (next(it) for _ in range(5))
    s0_ref = next(it) if has_state else None
    prev_refs = [next(it) for _ in range(n_prev)]
    y_ref = next(it)
    sfin_ref = next(it) if want_state else None
    o_scr, st_scr = next(it), next(it)
    hpb = HEADS_PER_STEP

    t_i = lax.broadcasted_iota(jnp.int32, (CHUNK, CHUNK), 0)
    s_i = lax.broadcasted_iota(jnp.int32, (CHUNK, CHUNK), 1)
    tris = [jnp.where(s_i <= t_i, 1.0, 0.0).astype(BF16), jnp.where(s_i >= t_i, 1.0, 0.0).astype(BF16)]
    tris = [jnp.concatenate([tr, tr, tr], axis=1) for tr in tris]
    masks = [_level_masks(False), _level_masks(True)]
    lf_refs = (lff_ref, lfb_ref)

    for hh in range(hpb):
        for d in range(2):
            if has_state:
                st_scr[2 * hh + d] = s0_ref[0, d, hh].T
            else:
                st_scr[2 * hh + d] = jnp.zeros((DV_A, DK_A), F32)

    def step(ci):
        for hh in range(hpb):
            ls = slice(hh * 128, (hh + 1) * 128)
            for d in range(2):
                c = ci if d == 0 else n_chunks - 1 - ci
                r0 = c * CHUNK
                if not isinstance(r0, int):
                    r0 = pl.multiple_of(r0, CHUNK)
                rows = pl.ds(r0, CHUNK)
                o, st = _hgrn_chunk(lf_refs[d][0, rows, ls], q_ref[0, rows, ls], v_ref[0, rows, ls],
                                    st_scr[2 * hh + d], tris[d], masks[d], reverse=(d == 1))
                o_scr[d, rows, ls] = o
                st_scr[2 * hh + d] = st

    if n_chunks <= 2:
        for ci in range(n_chunks):
            step(ci)
    else:
        def body(i, carry):
            for j in range(4):
                step(4 * i + j)
            return carry
        lax.fori_loop(0, n_chunks // 4, body, 0)

    for j, p_ref in enumerate(prev_refs):
        sfin_ref[0, j] = p_ref[0]
    for hh in range(hpb):
        ls = slice(hh * 128, (hh + 1) * 128)
        if want_state:
            for d in range(2):
                if n_prev:
                    sfin_ref[0, n_prev, d, hh] = st_scr[2 * hh + d].T
                else:
                    sfin_ref[0, d, hh] = st_scr[2 * hh + d].T
        o = o_scr[0, :, ls] + o_scr[1, :, ls]
        y_ref[0, :, ls] = o * lax.rsqrt(jnp.mean(o * o, axis=-1, keepdims=True) + EPS) * g_ref[0, :, ls]


def _hgrn(q3, lff3, lfb3, i3, hgrn_g, s0, want_state, prev_states=()):
    b, t, _ = q3.shape
    n_chunks = t // CHUNK
    has_state = s0 is not None
    hpb = HEADS_PER_STEP
    w = 128 * hpb
    n_prev = len(prev_states)
    col = pl.BlockSpec((1, t, w), lambda bi, hp: (bi, 0, hp))

    in_specs = [col, col, col, col, pl.BlockSpec((1, 1, w), lambda bi, hp: (hp, 0, 0))]
    args = [q3, lff3, lfb3, i3, hgrn_g.reshape(H_A // hpb, 1, w)]
    st_spec = pl.BlockSpec((1, 2, hpb, DK_A, DV_A), lambda bi, hp: (bi, 0, hp, 0, 0))
    if has_state:
        in_specs.append(st_spec)
        args.append(s0)
    in_specs += [st_spec] * n_prev
    args += list(prev_states)
    out_specs = [pl.BlockSpec((1, t, w), lambda bi, hp: (bi, 0, hp))]
    out_shape = [jax.ShapeDtypeStruct((b, t, H_A * DV_A), F32)]
    if want_state and n_prev:
        out_specs.append(pl.BlockSpec((1, n_prev + 1, 2, hpb, DK_A, DV_A),
                                      lambda bi, hp: (bi, 0, 0, hp, 0, 0)))
        out_shape.append(jax.ShapeDtypeStruct((b, n_prev + 1, 2, H_A, DK_A, DV_A), F32))
    elif want_state:
        out_specs.append(st_spec)
        out_shape.append(jax.ShapeDtypeStruct((b, 2, H_A, DK_A, DV_A), F32))
    res = pl.pallas_call(
        functools.partial(_hgrn_kernel, n_chunks=n_chunks, has_state=has_state,
                          want_state=want_state, n_prev=n_prev),
        grid=(b, H_A // hpb),
        in_specs=in_specs,
        out_specs=out_specs,
        out_shape=out_shape,
        scratch_shapes=[pltpu.VMEM((2, t, w), F32), pltpu.VMEM((2 * hpb, DV_A, DK_A), F32)],
        compiler_params=_cparams(("arbitrary", "arbitrary")),
        name="hgrn2",
    )(*args)
    return res if want_state else (res[0], None)


def _half_sum_matrix():
    r = lax.broadcasted_iota(jnp.int32, (128, 128), 0)
    c = lax.broadcasted_iota(jnp.int32, (128, 128), 1)
    return jnp.where((r >> 6) == (c >> 6), 1.0, 0.0).astype(BF16)


def _rms64(x, g, half_sum):
    x2 = x * x
    hi = x2.astype(BF16)
    lo = (x2 - hi.astype(F32)).astype(BF16)
    ss = _dot(hi, half_sum) + _dot(lo, half_sum)
    return x * lax.rsqrt(ss / DH_B + EPS) * g


def _rope(x, cos, sin):
    lane = lax.broadcasted_iota(jnp.int32, x.shape, 1)
    first = (lane & 31) < 16
    partner = jnp.where(first, pltpu.roll(x, 112, 1), pltpu.roll(x, 16, 1))
    return x * cos + partner * sin


def _attn_kernel(*refs, t_self, t_ctx, lam_init, latent, kv_out, n_prev):
    it = iter(refs)
    q_ref, k_ref, v_ref = (next(it) for _ in range(3))
    qg_ref, kg_ref, sg_ref, lam_ref = (next(it) for _ in range(4))
    if latent:
        cq_ref, sq_ref, ck_ref, sk_ref, kc_ref, vc_ref = (next(it) for _ in range(6))
    prev_k = [next(it) for _ in range(n_prev)]
    prev_v = [next(it) for _ in range(n_prev)]
    y_ref = next(it)
    kout_ref = next(it) if kv_out else None
    vout_ref = next(it) if kv_out == "stack" else None
    k_scr, v_scr = next(it), next(it)

    qi = pl.program_id(1)
    half_sum = _half_sum_matrix()
    t_all = t_self + t_ctx
    spb = q_ref.shape[0]

    @pl.when(qi == 0)
    def _():
        for sb in range(spb):
            for j in range(n_prev):
                kout_ref[sb, j] = prev_k[j][sb]
                vout_ref[sb, j] = prev_v[j][sb]
            if kv_out == "stack":
                vout_ref[sb, n_prev] = v_ref[sb]
            for h in range(H_B):
                ls = slice(h * 128, (h + 1) * 128)
                vs = slice(h * 256, h * 256 + 128)
                kn = _rms64(k_ref[sb, :, ls], kg_ref[...], half_sum)
                if kv_out == "stack":
                    kout_ref[sb, n_prev, :, ls] = kn
                elif kv_out:
                    kout_ref[sb, :, ls] = kn
                if latent:
                    kn = _rope(kn, ck_ref[...], sk_ref[...])
                    k_scr[sb, t_self:t_all, ls] = kc_ref[sb, 0, :, ls].astype(BF16)
                    v_scr[sb, t_self:t_all, vs] = vc_ref[sb, 0, :, ls].astype(BF16)
                k_scr[sb, 0:t_self, ls] = kn.astype(BF16)
                v_scr[sb, 0:t_self, vs] = v_ref[sb, :, ls].astype(BF16)
                v_scr[sb, :, h * 256 + 128:(h + 1) * 256] = jnp.ones((t_all, 128), BF16)

    lp = lam_ref[...]
    l01 = jnp.sum(lp[0:1] * lp[1:2], axis=-1, keepdims=True)
    l23 = jnp.sum(lp[2:3] * lp[3:4], axis=-1, keepdims=True)
    lam = jnp.exp(l01) - jnp.exp(l23) + lam_init

    for sb in range(spb):
        for h in range(H_B):
            ls = slice(h * 128, (h + 1) * 128)
            qn = _rms64(q_ref[sb, :, ls], qg_ref[...], half_sum)
            if latent:
                qn = _rope(qn, cq_ref[...], sq_ref[...])
            qn = qn * (DH_B ** -0.5)
            lane = lax.broadcasted_iota(jnp.int32, qn.shape, 1)
            kf = k_scr[sb, :, ls]
            v1 = v_scr[sb, :, h * 256:(h + 1) * 256]

            tq = qn.shape[0]
            qm = jnp.concatenate([jnp.where(lane < DH_B, qn, 0.0), jnp.where(lane < DH_B, 0.0, qn)],
                                 axis=0).astype(BF16)
            s = _dot_nt(qm, kf)
            e = jnp.exp(s - jnp.max(s, axis=-1, keepdims=True)).astype(BF16)
            oz = _dot(e, v1)
            o0, z0 = oz[0:tq, 0:128], oz[0:tq, 128:256]
            o1, z1 = oz[tq:2 * tq, 0:128], oz[tq:2 * tq, 128:256]
            o = o0 * (1.0 / z0) - o1 * (lam / z1)
            y = o * lax.rsqrt(jnp.mean(o * o, axis=-1, keepdims=True) + EPS) * sg_ref[...]
            y_ref[sb, :, ls] = y * (1.0 - lam_init)


def _diff_attn(q3, k3, v3, layer, qn_g, kn_g, sub_g, lam_p, rope_tab, cache_k, cache_v, kv_out,
               prev_k=(), prev_v=()):
    b, t, _ = q3.shape
    latent = rope_tab is not None
    tq = min(t, ATTN_QUERY_TILE)
    t_ctx = cache_k.shape[2] if latent else 0
    lam_init = 0.8 - 0.6 * math.exp(-0.3 * layer)
    n_prev = len(prev_k)
    w = H_B * 128

    def small(shape):
        return pl.BlockSpec(shape, lambda bi, qi: (0,) * len(shape))

    spb = 1 if latent else ATTN_SEQS_PER_STEP
    seq_blk = pl.BlockSpec((spb, t, w), lambda bi, qi: (bi, 0, 0))
    in_specs = [
        pl.BlockSpec((spb, tq, w), lambda bi, qi: (bi, qi, 0)),
        seq_blk, seq_blk,
        small((1, 128)), small((1, 128)), small((1, 128)), small((4, DH_B)),
    ]
    args = [q3, k3, v3, jnp.tile(qn_g, 2).reshape(1, 128), jnp.tile(kn_g, 2).reshape(1, 128),
            sub_g.reshape(1, 128), lam_p]
    if latent:
        cos_t, sin_t = rope_tab
        in_specs += [
            pl.BlockSpec((tq, 128), lambda bi, qi: (qi, 0)),
            pl.BlockSpec((tq, 128), lambda bi, qi: (qi, 0)),
            small((t, 128)), small((t, 128)),
            pl.BlockSpec((1, 1, t_ctx, w), lambda bi, qi: (bi, layer, 0, 0)),
            pl.BlockSpec((1, 1, t_ctx, w), lambda bi, qi: (bi, layer, 0, 0)),
        ]
        args += [cos_t, sin_t, cos_t, sin_t, cache_k, cache_v]
    in_specs += [seq_blk] * (2 * n_prev)
    args += list(prev_k) + list(prev_v)
    out_specs = [pl.BlockSpec((spb, tq, w), lambda bi, qi: (bi, qi, 0))]
    out_shape = [jax.ShapeDtypeStruct((b, t, w), F32)]
    if kv_out == "stack":
        stack_spec = pl.BlockSpec((spb, n_prev + 1, t, w), lambda bi, qi: (bi, 0, 0, 0))
        out_specs += [stack_spec, stack_spec]
        out_shape += [jax.ShapeDtypeStruct((b, n_prev + 1, t, w), F32)] * 2
    elif kv_out:
        out_specs.append(seq_blk)
        out_shape.append(jax.ShapeDtypeStruct((b, t, w), F32))
    return pl.pallas_call(
        functools.partial(_attn_kernel, t_self=t, t_ctx=t_ctx, lam_init=lam_init,
                          latent=latent, kv_out=kv_out, n_prev=n_prev),
        grid=(b // spb, t // tq),
        in_specs=in_specs,
        out_specs=out_specs,
        out_shape=out_shape,
        scratch_shapes=[pltpu.VMEM((spb, t + t_ctx, w), BF16),
                        pltpu.VMEM((spb, t + t_ctx, 2 * w), BF16)],
        compiler_params=_cparams(("arbitrary", "arbitrary")),
        name="diff_attn",
    )(*args)


def _dft_tables(t):
    def cs(n):
        idx = np.arange(n, dtype=np.int64)
        ang = 2.0 * np.pi * ((idx[:, None] * idx[None, :]) % n).astype(np.float64) / n
        return np.cos(ang), np.sin(ang)

    cc, sc = cs(DG_C)
    ct, st = cs(t)
    chan = jnp.asarray(np.concatenate([cc, sc], axis=1), dtype=F32)
    pos = jnp.asarray(np.concatenate([ct, -st], axis=1), dtype=F32)
    return chan, pos


def _fourier_kernel(u_ref, chan_ref, pos_ref, y_ref, uc_scr, *, t, scale):
    ri = pl.program_id(1)

    spb, _, w = u_ref.shape

    @pl.when(ri == 0)
    def _():
        for sb in range(spb):
            for g in range(G_C):
                ls = slice(g * DG_C, (g + 1) * DG_C)
                cs = slice(sb * w + g * DG_C, sb * w + (g + 1) * DG_C)
                u1 = _dot(u_ref[sb, :, ls].astype(BF16), chan_ref[...].astype(BF16))
                uc_scr[0:t, cs] = u1[:, 0:DG_C].astype(BF16)
                uc_scr[t:2 * t, cs] = u1[:, DG_C:2 * DG_C].astype(BF16)

    f = _dot(pos_ref[...].astype(BF16), uc_scr[...]) * scale
    for sb in range(spb):
        y_ref[sb] = f[:, sb * w:(sb + 1) * w]


def _fourier(pc3):
    b, t, w = pc3.shape
    tr = min(t, 512)
    spb = FOURIER_SEQS_PER_STEP if t // tr == 1 else 1
    chan, pos = _dft_tables(t)
    scale = 1.0 / math.sqrt(t * DG_C)
    return pl.pallas_call(
        functools.partial(_fourier_kernel, t=t, scale=scale),
        grid=(b // spb, t // tr),
        in_specs=[
            pl.BlockSpec((spb, t, w), lambda bi, ri: (bi, 0, 0)),
            _resident((DG_C, 2 * DG_C)),
            pl.BlockSpec((tr, 2 * t), lambda bi, ri: (ri, 0)),
        ],
        out_specs=pl.BlockSpec((spb, tr, w), lambda bi, ri: (bi, ri, 0)),
        out_shape=jax.ShapeDtypeStruct((b, t, w), F32),
        scratch_shapes=[pltpu.VMEM((2 * t, spb * w), BF16)],
        compiler_params=_cparams(("arbitrary", "arbitrary")),
        name="fourier_mix",
    )(pc3, chan, pos)


def _merge_kernel(x_ref, mod_ref, g_ref, oa_ref, ob_ref, oc_ref, *refs):
    n_seg = len(MERGE_SEGS)
    wm_refs = refs[:n_seg]
    wa_ref, wb_ref, wc_ref, wo_ref, o_ref = refs[n_seg:]
    x = x_ref[...]
    h = _modulated_norm(x, g_ref[...], mod_ref)

    def seg(s):
        return _dot(h, wm_refs[s][0])

    ya = (oa_ref[...] * _silu(seg(0))).astype(BF16)
    yb = (ob_ref[...] * _silu(seg(1))).astype(BF16)
    yc = (oc_ref[...] * _silu(seg(2))).astype(BF16)
    parts = []
    for n in range(2):
        cols = slice(n * SEG, (n + 1) * SEG)
        m = jax.nn.sigmoid(seg(3 + n)) * _dot(ya, wa_ref[:, cols])
        m = m + jax.nn.sigmoid(seg(5 + n)) * _dot(yb, wb_ref[:, cols])
        m = m + jax.nn.sigmoid(seg(7 + n)) * _dot(yc, wc_ref[:, cols])
        parts.append(m.astype(BF16))
    merged = jnp.concatenate(parts, axis=1)
    gate = mod_ref[0, 2:3, :]
    o_ref[...] = x + gate * _dot(merged, wo_ref[...])


def _merge(x2, mod3, row_of_tile, norm_g, oa, ob, oc, w_in_bf, layer, wa, wb, wc, wo):
    n = x2.shape[0]
    tm = TOKEN_TILE
    tok = lambda i: (i, 0)
    in_specs = [
        pl.BlockSpec((tm, D_MODEL), tok),
        pl.BlockSpec((1, 3, D_MODEL), lambda i: (row_of_tile(i), 0, 0)),
        _resident((1, D_MODEL)),
        pl.BlockSpec((tm, SEG), tok), pl.BlockSpec((tm, SEG), tok), pl.BlockSpec((tm, SEG), tok),
    ] + _segment_specs(layer, MERGE_SEGS) + [
        _resident((SEG, D_MODEL)), _resident((SEG, D_MODEL)), _resident((SEG, D_MODEL)),
        _resident((D_MODEL, D_MODEL)),
    ]
    return pl.pallas_call(
        _merge_kernel,
        grid=(n // tm,),
        in_specs=in_specs,
        out_specs=pl.BlockSpec((tm, D_MODEL), tok),
        out_shape=jax.ShapeDtypeStruct((n, D_MODEL), F32),
        compiler_params=_cparams(("arbitrary",)),
        name="merge_out",
    )(x2, mod3, norm_g.reshape(1, D_MODEL), oa, ob, oc, *([w_in_bf] * len(MERGE_SEGS)),
      wa, wb, wc, wo)


def _rope_tables(t):
    nf = DH_B // 4
    n_rows = t // GRID_W
    pos_row = jnp.repeat(jnp.arange(n_rows, dtype=F32), GRID_W)
    pos_col = jnp.tile(jnp.arange(GRID_W, dtype=F32), n_rows)
    inv = ROPE_BASE ** (-jnp.arange(nf, dtype=F32) / nf)
    ar = pos_row[:, None] * inv[None, :]
    ac = pos_col[:, None] * inv[None, :]
    cos64 = jnp.concatenate([jnp.cos(ar), jnp.cos(ar), jnp.cos(ac), jnp.cos(ac)], axis=1)
    sin64 = jnp.concatenate([-jnp.sin(ar), jnp.sin(ar), -jnp.sin(ac), jnp.sin(ac)], axis=1)
    return jnp.tile(cos64, (1, 2)), jnp.tile(sin64, (1, 2))


def _layer(x, mod3, row_of_tile, layer, lw, log_lb, log1m_lb, ctx=None, rope_tab=None, prev=None):
    (norm_g, w_in_bf, hgrn_g, qn_g, kn_g, lam_p, sub_g, wa, wb, wc, wo) = lw
    b, t, _ = x.shape
    x2 = x.reshape(b * t, D_MODEL)
    segs = _in_proj(x2, mod3, row_of_tile, norm_g, w_in_bf, layer, log_lb, log1m_lb)
    qa, lff, lfb, ia, qb, kb, vb, uc = (s.reshape(b, t, SEG) for s in segs)
    attn = functools.partial(_diff_attn, qb, kb, vb, layer, qn_g, kn_g, sub_g, lam_p)
    is_ctx = ctx is None
    if not is_ctx:
        oa, _ = _hgrn(qa, lff, lfb, ia, hgrn_g, ctx[2], want_state=False)
        (ob,) = attn(rope_tab, ctx[0], ctx[1], None)
    elif prev is None:
        oa, s_out = _hgrn(qa, lff, lfb, ia, hgrn_g, None, want_state=True)
        ob, k_out = attn(None, None, None, "plain")
        v_out = vb
    else:
        oa, s_out = _hgrn(qa, lff, lfb, ia, hgrn_g, None, want_state=True, prev_states=prev[2])
        ob, k_out, v_out = attn(None, None, None, "stack", prev_k=prev[0], prev_v=prev[1])
    oc = _fourier(uc)
    x_new = _merge(x2, mod3, row_of_tile, norm_g, oa.reshape(b * t, SEG), ob.reshape(b * t, SEG),
                   oc.reshape(b * t, SEG), w_in_bf, layer, wa, wb, wc, wo)
    x_new = x_new.reshape(b, t, D_MODEL)
    if is_ctx:
        return x_new, k_out, v_out, s_out
    return x_new


def kernel(x_prompt, x_sample, c, cache_diff_k, cache_diff_v, state_hgrn, c_ctx, ada_w, ada_b,
           norm_g, w_in, hgrn_lb, hgrn_norm_g, diff_qn_g, diff_kn_g, diff_lambda, diff_subln_g,
           w_branch_a, w_branch_b, w_branch_c, w_out):
    batch, seq, _ = x_prompt.shape
    dec_b, dec_t, _ = x_sample.shape
    past = cache_diff_k.shape[2]

    lb_all = jnp.cumsum(jax.nn.softmax(hgrn_lb.astype(F32), axis=0), axis=0)
    lb_all = lb_all - lb_all[0]
    log_lb = jnp.log(lb_all)
    log1m_lb = jnp.log1p(-lb_all)

    cond8 = jnp.zeros((8, D_MODEL), F32).at[0].set(c_ctx).at[1:1 + dec_b].set(c)
    mod = _modulation(cond8, ada_w, ada_b).reshape(DEPTH, 8, 3, D_MODEL)

    rope_tab = _rope_tables(dec_t)
    cache_k4 = cache_diff_k.reshape(dec_b, DEPTH, past, H_B * 2 * DH_B)
    cache_v4 = cache_diff_v.reshape(dec_b, DEPTH, past, H_B * DV_B)
    tiles_per_seq = dec_t // TOKEN_TILE
    ctx_row = lambda i: 0
    lat_row = lambda i: 1 + i // tiles_per_seq

    xp, xs = x_prompt, x_sample
    prev_k, prev_pb, prev_s = [], [], []
    w_in_bf = w_in.astype(BF16)
    for l in range(DEPTH):
        lw = (norm_g[l], w_in_bf,
              hgrn_norm_g[l], diff_qn_g[l], diff_kn_g[l], diff_lambda[l], diff_subln_g[l],
              w_branch_a[l].astype(BF16), w_branch_b[l].astype(BF16),
              w_branch_c[l].astype(BF16), w_out[l].astype(BF16))
        mod3 = mod[l]
        last = l == DEPTH - 1
        xp, k_c, v_c, s_c = _layer(xp, mod3, ctx_row, l, lw, log_lb[l], log1m_lb[l],
                                   prev=(prev_k, prev_pb, prev_s) if last and l > 0 else None)
        if not last:
            prev_k.append(k_c)
            prev_pb.append(v_c)
            prev_s.append(s_c)
        xs = _layer(xs, mod3, lat_row, l, lw, log_lb[l], log1m_lb[l],
                    ctx=(cache_k4, cache_v4, state_hgrn[:, l]), rope_tab=rope_tab)
    new_k = k_c.reshape(batch, DEPTH, seq, H_B, 2, DH_B)
    new_v = v_c.reshape(batch, DEPTH, seq, H_B, DV_B)
    return (xp, xs, new_k, new_v, s_c)
```
